```python
import jax, jax.numpy as jnp
from jax import lax
import numpy as np

D_MODEL = 1024
BATCH = 16
SEQ = 2048
DEPTH = 4
DEC_BATCH = 16
DEC_SEQ = 32
PAST_LEN = 4096

CHUNK = 64
N_MIXERS = 2
N_ATTN_LAYERS = (DEPTH + 1) // 2
N_CONV_LAYERS = DEPTH // 2
N_HEADS = 16
N_KV_HEADS = 4
HEAD_DIM = D_MODEL // N_HEADS
GROUP = N_HEADS // N_KV_HEADS
Q_DIM = N_HEADS * HEAD_DIM
KV_DIM = N_KV_HEADS * HEAD_DIM
QKV_DIM = Q_DIM + 2 * KV_DIM
WINDOW = 128
WINDOW_CHUNKS = WINDOW // CHUNK
BAND = WINDOW + CHUNK
CONV_WIDTH = 31
CONV_PAD = CONV_WIDTH - 1
FFN_DIM = 2816
NORM_EPS = 1e-5
NEG_INF = -1e30

kernel_name = 'streaming_swa_sink_conformer_hybrid'


def _rmsnorm(x, g):
    xf = x.astype(jnp.float32)
    y = xf * lax.rsqrt(jnp.mean(xf * xf, axis=-1, keepdims=True) + NORM_EPS)
    return (y * g.astype(jnp.float32)).astype(x.dtype)


def _layernorm(x, g, b):
    xf = x.astype(jnp.float32)
    mu = jnp.mean(xf, axis=-1, keepdims=True)
    xc = xf - mu
    y = xc * lax.rsqrt(jnp.mean(xc * xc, axis=-1, keepdims=True) + NORM_EPS)
    return y * g.astype(jnp.float32) + b.astype(jnp.float32)


def _swiglu(h, w_in, w_out):
    gu = h @ w_in
    g, u = gu[..., :FFN_DIM], gu[..., FFN_DIM:]
    return (jax.nn.silu(g) * u) @ w_out


def _alibi_bias(tq, tk):
    slopes = jnp.asarray(np.array([2.0 ** (-8.0 * (h + 1) / N_HEADS) for h in range(N_HEADS)], dtype=np.float32))
    i = jnp.arange(tq)[:, None]
    j = jnp.arange(tk)[None, :]
    dist = jnp.abs(i - (j - WINDOW)).astype(jnp.float32)
    return (-slopes[:, None, None] * dist).reshape(N_KV_HEADS, GROUP, tq, tk)


def _qkv(h, w_qkv, b_qkv):
    p = h @ w_qkv + b_qkv
    lead = h.shape[:-1]
    q = p[..., :Q_DIM].reshape(*lead, N_KV_HEADS, GROUP, HEAD_DIM)
    k = p[..., Q_DIM:Q_DIM + KV_DIM].reshape(*lead, N_KV_HEADS, HEAD_DIM)
    v = p[..., Q_DIM + KV_DIM:].reshape(*lead, N_KV_HEADS, HEAD_DIM)
    return q, k, v


def _attend(q, k, v, bias, mask, sinks):
    s = jnp.einsum('...qkgd,...skd->...kgqs', q, k).astype(jnp.float32) * (HEAD_DIM ** -0.5) + bias
    if mask is not None:
        s = jnp.where(mask, s, NEG_INF)
    sink = sinks.astype(jnp.float32).reshape(N_KV_HEADS, GROUP, 1, 1)
    m = jnp.maximum(jnp.max(s, axis=-1, keepdims=True), sink)
    p = jnp.exp(s - m)
    denom = jnp.sum(p, axis=-1, keepdims=True) + jnp.exp(sink - m)
    return jnp.einsum('...kgqs,...skd->...qkgd', (p / denom).astype(v.dtype), v)


def _attn_prompt(h, w_qkv, b_qkv, w_o, sinks):
    b, s, _ = h.shape
    nc = s // CHUNK
    q, k, v = _qkv(h, w_qkv, b_qkv)
    pad = ((0, 0), (WINDOW, 0), (0, 0), (0, 0))
    kp = jnp.pad(k, pad).reshape(b, nc + WINDOW_CHUNKS, CHUNK, N_KV_HEADS, HEAD_DIM)
    vp = jnp.pad(v, pad).reshape(b, nc + WINDOW_CHUNKS, CHUNK, N_KV_HEADS, HEAD_DIM)
    kb = jnp.concatenate([kp[:, j:j + nc] for j in range(WINDOW_CHUNKS + 1)], axis=2)
    vb = jnp.concatenate([vp[:, j:j + nc] for j in range(WINDOW_CHUNKS + 1)], axis=2)
    qb = q.reshape(b, nc, CHUNK, N_KV_HEADS, GROUP, HEAD_DIM)
    key_pos = jnp.arange(nc)[:, None] * CHUNK + jnp.arange(BAND)[None, :] - WINDOW
    mask = (key_pos >= 0)[:, None, None, None, :]
    o = _attend(qb, kb, vb, _alibi_bias(CHUNK, BAND), mask, sinks)
    out = o.reshape(b, s, Q_DIM) @ w_o
    return out, k[:, -WINDOW:], v[:, -WINDOW:]


def _attn_sample(h, ck, cv, w_qkv, b_qkv, w_o, sinks):
    b, t, _ = h.shape
    q, k, v = _qkv(h, w_qkv, b_qkv)
    kk = jnp.concatenate([ck.astype(k.dtype), k], axis=1)
    vv = jnp.concatenate([cv.astype(v.dtype), v], axis=1)
    o = _attend(q, kk, vv, _alibi_bias(t, WINDOW + t), None, sinks)
    out = o.reshape(b, t, Q_DIM) @ w_o
    return out, kk[:, -WINDOW:], vv[:, -WINDOW:]


def _conv_module(h, buf, w_in, b_in, w_dw, b_dw, ln_g, ln_b, w_out, b_out):
    ag = h @ w_in + b_in
    u = ag[..., :D_MODEL] * jax.nn.sigmoid(ag[..., D_MODEL:])
    up = jnp.concatenate([buf.astype(u.dtype), u], axis=1)
    y = lax.conv_general_dilated(up, w_dw.astype(up.dtype)[:, None, :], window_strides=(1,), padding='VALID',
                                 dimension_numbers=('NWC', 'WIO', 'NWC'), feature_group_count=D_MODEL) + b_dw
    y = jax.nn.silu(_layernorm(y, ln_g, ln_b)).astype(h.dtype)
    return y @ w_out + b_out, up[:, -CONV_PAD:]


def _trunk(x, cache_k, cache_v, state_conv, norm_ffn1, norm_mix, norm_ffn2, norm_final,
           ffn1_w_in, ffn1_w_out, ffn2_w_in, ffn2_w_out, attn_w_qkv, attn_b_qkv, attn_w_o, attn_sinks,
           conv_w_in, conv_b_in, conv_w_dw, conv_b_dw, conv_ln_g, conv_ln_b, conv_w_out, conv_b_out):
    prompt = cache_k is None
    new_k, new_v, new_conv = [], [], []
    for layer in range(DEPTH):
        x = x + 0.5 * _swiglu(_rmsnorm(x, norm_ffn1[layer]), ffn1_w_in[layer], ffn1_w_out[layer])
        h = _rmsnorm(x, norm_mix[layer])
        if layer % N_MIXERS == 0:
            a = layer // N_MIXERS
            if prompt:
                out, k, v = _attn_prompt(h, attn_w_qkv[a], attn_b_qkv[a], attn_w_o[a], attn_sinks[a])
            else:
                out, k, v = _attn_sample(h, cache_k[a], cache_v[a], attn_w_qkv[a], attn_b_qkv[a], attn_w_o[a], attn_sinks[a])
            new_k.append(k)
            new_v.append(v)
        else:
            c = layer // N_MIXERS
            buf = jnp.zeros((x.shape[0], CONV_PAD, D_MODEL), x.dtype) if prompt else state_conv[c]
            out, st = _conv_module(h, buf, conv_w_in[c], conv_b_in[c], conv_w_dw[c], conv_b_dw[c],
                                   conv_ln_g[c], conv_ln_b[c], conv_w_out[c], conv_b_out[c])
            new_conv.append(st)
        x = x + out
        x = x + 0.5 * _swiglu(_rmsnorm(x, norm_ffn2[layer]), ffn2_w_in[layer], ffn2_w_out[layer])
    y = _rmsnorm(x, norm_final)
    return y, jnp.stack(new_k), jnp.stack(new_v), jnp.stack(new_conv)


def setup_inputs(seed: int = 0) -> dict:
    key = jax.random.key(seed)
    ks = jax.random.split(key, 32)
    f32 = jnp.float32
    nrm = lambda k, shape, scale: (jax.random.normal(k, shape, f32) * scale)
    return {
        'x_prompt': nrm(ks[0], (BATCH, SEQ, D_MODEL), 1.0),
        'x_sample': nrm(ks[1], (DEC_BATCH, DEC_SEQ, D_MODEL), 1.0),
        'cache_k': nrm(ks[2], (N_ATTN_LAYERS, DEC_BATCH, WINDOW, N_KV_HEADS, HEAD_DIM), 1.0),
        'cache_v': nrm(ks[3], (N_ATTN_LAYERS, DEC_BATCH, WINDOW, N_KV_HEADS, HEAD_DIM), 1.0),
        'state_conv': nrm(ks[4], (N_CONV_LAYERS, DEC_BATCH, CONV_PAD, D_MODEL), 1.0),
        'norm_ffn1': 1.0 + nrm(ks[5], (DEPTH, D_MODEL), 0.02),
        'norm_mix': 1.0 + nrm(ks[6], (DEPTH, D_MODEL), 0.02),
        'norm_ffn2': 1.0 + nrm(ks[7], (DEPTH, D_MODEL), 0.02),
        'norm_final': 1.0 + nrm(ks[8], (D_MODEL,), 0.02),
        'ffn1_w_in': nrm(ks[9], (DEPTH, D_MODEL, 2 * FFN_DIM), D_MODEL ** -0.5),
        'ffn1_w_out': nrm(ks[10], (DEPTH, FFN_DIM, D_MODEL), FFN_DIM ** -0.5),
        'ffn2_w_in': nrm(ks[11], (DEPTH, D_MODEL, 2 * FFN_DIM), D_MODEL ** -0.5),
        'ffn2_w_out': nrm(ks[12], (DEPTH, FFN_DIM, D_MODEL), FFN_DIM ** -0.5),
        'attn_w_qkv': nrm(ks[13], (N_ATTN_LAYERS, D_MODEL, QKV_DIM), D_MODEL ** -0.5),
        'attn_b_qkv': nrm(ks[14], (N_ATTN_LAYERS, QKV_DIM), 0.01),
        'attn_w_o': nrm(ks[15], (N_ATTN_LAYERS, Q_DIM, D_MODEL), Q_DIM ** -0.5),
        'attn_sinks': nrm(ks[16], (N_ATTN_LAYERS, N_HEADS), 0.5),
        'conv_w_in': nrm(ks[17], (N_CONV_LAYERS, D_MODEL, 2 * D_MODEL), D_MODEL ** -0.5),
        'conv_b_in': nrm(ks[18], (N_CONV_LAYERS, 2 * D_MODEL), 0.01),
        'conv_w_dw': nrm(ks[19], (N_CONV_LAYERS, CONV_WIDTH, D_MODEL), CONV_WIDTH ** -0.5),
        'conv_b_dw': nrm(ks[20], (N_CONV_LAYERS, D_MODEL), 0.01),
        'conv_ln_g': 1.0 + nrm(ks[21], (N_CONV_LAYERS, D_MODEL), 0.02),
        'conv_ln_b': nrm(ks[22], (N_CONV_LAYERS, D_MODEL), 0.01),
        'conv_w_out': nrm(ks[23], (N_CONV_LAYERS, D_MODEL, D_MODEL), D_MODEL ** -0.5),
        'conv_b_out': nrm(ks[24], (N_CONV_LAYERS, D_MODEL), 0.01),
    }


def reference(x_prompt, x_sample, cache_k, cache_v, state_conv, norm_ffn1, norm_mix, norm_ffn2, norm_final,
              ffn1_w_in, ffn1_w_out, ffn2_w_in, ffn2_w_out, attn_w_qkv, attn_b_qkv, attn_w_o, attn_sinks,
              conv_w_in, conv_b_in, conv_w_dw, conv_b_dw, conv_ln_g, conv_ln_b, conv_w_out, conv_b_out):
    y_prompt, new_k_prompt, new_v_prompt, new_conv_prompt = _trunk(
        x_prompt, None, None, None, norm_ffn1, norm_mix, norm_ffn2, norm_final,
        ffn1_w_in, ffn1_w_out, ffn2_w_in, ffn2_w_out, attn_w_qkv, attn_b_qkv, attn_w_o, attn_sinks,
        conv_w_in, conv_b_in, conv_w_dw, conv_b_dw, conv_ln_g, conv_ln_b, conv_w_out, conv_b_out)
    y_sample, new_k_sample, new_v_sample, new_conv_sample = _trunk(
        x_sample, cache_k, cache_v, state_conv, norm_ffn1, norm_mix, norm_ffn2, norm_final,
        ffn1_w_in, ffn1_w_out, ffn2_w_in, ffn2_w_out, attn_w_qkv, attn_b_qkv, attn_w_o, attn_sinks,
        conv_w_in, conv_b_in, conv_w_dw, conv_b_dw, conv_ln_g, conv_ln_b, conv_w_out, conv_b_out)
    return (y_prompt, y_sample, new_k_prompt, new_v_prompt, new_conv_prompt, new_k_sample, new_v_sample, new_conv_sample)
```

```python
import functools

import numpy as np
import jax
import jax.numpy as jnp
from jax import lax
from jax.experimental import pallas as pl
from jax.experimental.pallas import tpu as pltpu

D_MODEL = 1024
BATCH = 16
SEQ = 2048
DEPTH = 4
DEC_BATCH = 16
DEC_SEQ = 32
CHUNK = 64
N_HEADS = 16
N_KV_HEADS = 4
HEAD_DIM = 64
GROUP = N_HEADS // N_KV_HEADS
Q_DIM = N_HEADS * HEAD_DIM
KV_DIM = N_KV_HEADS * HEAD_DIM
QKV_DIM = Q_DIM + 2 * KV_DIM
WINDOW = 128
BAND = WINDOW + CHUNK
CONV_WIDTH = 31
CONV_PAD = CONV_WIDTH - 1
FFN_DIM = 2816
NORM_EPS = 1e-5
NEG_INF = -1e30

F32 = jnp.float32
BF16 = jnp.bfloat16

LANES = 128
PROMPT_ROWS = BATCH * SEQ
SAMPLE_ROWS = DEC_BATCH * DEC_SEQ
ROWS = PROMPT_ROWS + SAMPLE_ROWS
FFN_TILE = 512
FFN_CHUNK = 256
MIX_TILE = 512
HIST = 32
CONV_ROWS = 64
VMEM_LIMIT = 56 * 1024 * 1024

assert ROWS % FFN_TILE == 0 and SEQ % MIX_TILE == 0 and MIX_TILE % CHUNK == 0
assert FFN_DIM % FFN_CHUNK == 0 and HIST >= CONV_PAD and DEC_SEQ == HIST


def _rmsnorm(x, g):
    return x * lax.rsqrt(jnp.mean(x * x, axis=-1, keepdims=True) + NORM_EPS) * g


def _dot(a, b):
    return jnp.dot(a, b, preferred_element_type=F32)


def _dot_nt(a, b):
    return lax.dot_general(a, b, (((1,), (1,)), ((), ())), preferred_element_type=F32)


def _const_spec(shape):
    return pl.BlockSpec(shape, lambda *_: (0,) * len(shape), pipeline_mode=pl.Buffered(1))


def _ffn_body(x_ref, g_ref, win_ref, wout_ref, gfin_ref, o_ref, act_ref, *, final):
    x = x_ref[...]
    h = _rmsnorm(x, g_ref[...]).astype(BF16)
    for c in range(FFN_DIM // FFN_CHUNK):
        lo = c * FFN_CHUNK
        gate = _dot(h, win_ref[:, lo:lo + FFN_CHUNK])
        up = _dot(h, win_ref[:, FFN_DIM + lo:FFN_DIM + lo + FFN_CHUNK])
        act_ref[:, lo:lo + FFN_CHUNK] = (gate * jax.nn.sigmoid(gate) * up).astype(BF16)
    y = x + 0.5 * _dot(act_ref[...], wout_ref[...])
    if final:
        y = _rmsnorm(y, gfin_ref[...])
    o_ref[...] = y


def _ffn(x, g, w_in, w_out, g_final, final):
    return pl.pallas_call(
        functools.partial(_ffn_body, final=final),
        grid=(ROWS // FFN_TILE,),
        in_specs=[
            pl.BlockSpec((FFN_TILE, D_MODEL), lambda i: (i, 0)),
            _const_spec((1, D_MODEL)),
            _const_spec((D_MODEL, 2 * FFN_DIM)),
            _const_spec((FFN_DIM, D_MODEL)),
            _const_spec((1, D_MODEL)),
        ],
        out_specs=pl.BlockSpec((FFN_TILE, D_MODEL), lambda i: (i, 0)),
        out_shape=jax.ShapeDtypeStruct((ROWS, D_MODEL), F32),
        scratch_shapes=[pltpu.VMEM((FFN_TILE, FFN_DIM), BF16)],
        compiler_params=pltpu.CompilerParams(
            dimension_semantics=("arbitrary",), vmem_limit_bytes=VMEM_LIMIT),
        name="ffn_final" if final else "ffn",
    )(x, g, w_in, w_out, g_final)


HEAD_ORDER = (0, 2, 1, 3)


def _split_heads(kv, klo_ref, khi_ref, v2_ref, row0, k_not_v):
    rows = kv.shape[0]
    lane = lax.broadcasted_iota(jnp.int32, (rows, LANES), 1)
    for p in range(KV_DIM // LANES):
        pair = kv[:, p * LANES:(p + 1) * LANES]
        lo_even = jnp.where(lane < HEAD_DIM, pair, 0.0)
        hi_odd = jnp.where(lane >= HEAD_DIM, pair, 0.0)
        hi_even = pltpu.roll(lo_even, HEAD_DIM, 1)
        lo_odd = pltpu.roll(hi_odd, HEAD_DIM, 1)
        for kh, lo, hi in ((2 * p, lo_even, hi_even), (2 * p + 1, lo_odd, hi_odd)):
            if k_not_v:
                klo_ref[kh, row0:row0 + rows, :] = lo.astype(BF16)
                khi_ref[kh, row0:row0 + rows, :] = hi.astype(BF16)
            else:
                v2_ref[kh, row0:row0 + rows, 0:LANES] = lo.astype(BF16)
                v2_ref[kh, row0:row0 + rows, LANES:2 * LANES] = hi.astype(BF16)


def _attend_chunk(q_c, key0, nkeys, klo_ref, khi_ref, v2_ref, bias_ref, sink_ref, valid):
    t = q_c.shape[0]
    pieces = []
    for kh in range(N_KV_HEADS):
        col = kh * 2 * LANES
        q2 = jnp.concatenate([q_c[:, col:col + LANES], q_c[:, col + LANES:col + 2 * LANES]], axis=0)
        s_even = _dot_nt(q2, klo_ref[kh, key0:key0 + nkeys, :])
        s_odd = _dot_nt(q2, khi_ref[kh, key0:key0 + nkeys, :])
        s = jnp.concatenate([s_even, s_odd], axis=0) + bias_ref[kh]
        if valid is not None:
            s = jnp.where(valid, s, NEG_INF)
        sink = sink_ref[kh]
        m = jnp.maximum(jnp.max(s, axis=-1, keepdims=True), sink)
        p = jnp.exp(s - m)
        den = jnp.sum(p, axis=-1, keepdims=True) + jnp.exp(sink - m)
        o2 = _dot(p.astype(BF16), v2_ref[kh, key0:key0 + nkeys, :]) / den
        pieces.append(o2[0:t, 0:LANES] + o2[2 * t:3 * t, LANES:2 * LANES])
        pieces.append(o2[t:2 * t, 0:LANES] + o2[3 * t:4 * t, LANES:2 * LANES])
    return jnp.concatenate(pieces, axis=1)


def _attn_prompt_body(x_ref, g_ref, wqkv_ref, bqkv_ref, wo_ref, bias_ref, sink_ref,
                      o_ref, nk_ref, nv_ref, q_ref, klo_ref, khi_ref, v2_ref, att_ref):
    t = pl.program_id(1)

    @pl.when(t == 0)
    def _():
        klo_ref[:, 0:WINDOW, :] = jnp.zeros((N_KV_HEADS, WINDOW, LANES), BF16)
        khi_ref[:, 0:WINDOW, :] = jnp.zeros((N_KV_HEADS, WINDOW, LANES), BF16)
        v2_ref[:, 0:WINDOW, :] = jnp.zeros((N_KV_HEADS, WINDOW, 2 * LANES), BF16)

    x = x_ref[...]
    h = _rmsnorm(x, g_ref[...]).astype(BF16)
    qkv = _dot(h, wqkv_ref[...]) + bqkv_ref[...]
    q_ref[...] = (qkv[:, :Q_DIM] * (HEAD_DIM ** -0.5)).astype(BF16)
    k = qkv[:, Q_DIM:Q_DIM + KV_DIM]
    v = qkv[:, Q_DIM + KV_DIM:]
    nk_ref[...] = k[MIX_TILE - WINDOW:]
    nv_ref[...] = v[MIX_TILE - WINDOW:]
    _split_heads(k, klo_ref, khi_ref, v2_ref, WINDOW, True)
    _split_heads(v, klo_ref, khi_ref, v2_ref, WINDOW, False)

    key_idx = lax.broadcasted_iota(jnp.int32, (GROUP * CHUNK, BAND), 1)
    for c in range(MIX_TILE // CHUNK):
        first_valid = WINDOW - CHUNK * (t * (MIX_TILE // CHUNK) + c)
        valid = key_idx >= first_valid
        q_c = q_ref[c * CHUNK:(c + 1) * CHUNK, :]
        att = _attend_chunk(q_c, c * CHUNK, BAND, klo_ref, khi_ref, v2_ref, bias_ref, sink_ref, valid)
        att_ref[c * CHUNK:(c + 1) * CHUNK, :] = att.astype(BF16)

    o_ref[...] = x + _dot(att_ref[...], wo_ref[...])
    klo_ref[:, 0:WINDOW, :] = klo_ref[:, MIX_TILE:MIX_TILE + WINDOW, :]
    khi_ref[:, 0:WINDOW, :] = khi_ref[:, MIX_TILE:MIX_TILE + WINDOW, :]
    v2_ref[:, 0:WINDOW, :] = v2_ref[:, MIX_TILE:MIX_TILE + WINDOW, :]


def _attn_prompt(x, g, w_qkv, b_qkv, w_o, bias, sinks):
    tiles = SEQ // MIX_TILE
    kv_shape = jax.ShapeDtypeStruct((BATCH, WINDOW, KV_DIM), F32)
    kv_spec = pl.BlockSpec((None, WINDOW, KV_DIM), lambda b, t: (b, 0, 0))
    return pl.pallas_call(
        _attn_prompt_body,
        grid=(BATCH, tiles),
        in_specs=[
            pl.BlockSpec((MIX_TILE, D_MODEL), lambda b, t: (b * tiles + t, 0)),
            _const_spec((1, D_MODEL)),
            _const_spec((D_MODEL, QKV_DIM)),
            _const_spec((1, QKV_DIM)),
            _const_spec((Q_DIM, D_MODEL)),
            _const_spec((N_KV_HEADS, GROUP * CHUNK, BAND)),
            _const_spec((N_KV_HEADS, GROUP * CHUNK, 1)),
        ],
        out_specs=[
            pl.BlockSpec((MIX_TILE, D_MODEL), lambda b, t: (b * tiles + t, 0)),
            kv_spec, kv_spec,
        ],
        out_shape=[jax.ShapeDtypeStruct((ROWS, D_MODEL), F32), kv_shape, kv_shape],
        scratch_shapes=[
            pltpu.VMEM((MIX_TILE, Q_DIM), BF16),
            pltpu.VMEM((N_KV_HEADS, WINDOW + MIX_TILE, LANES), BF16),
            pltpu.VMEM((N_KV_HEADS, WINDOW + MIX_TILE, LANES), BF16),
            pltpu.VMEM((N_KV_HEADS, WINDOW + MIX_TILE, 2 * LANES), BF16),
            pltpu.VMEM((MIX_TILE, Q_DIM), BF16),
        ],
        input_output_aliases={0: 0},
        compiler_params=pltpu.CompilerParams(
            dimension_semantics=("arbitrary", "arbitrary"), vmem_limit_bytes=VMEM_LIMIT),
        name="attn_prompt",
    )(x, g, w_qkv, b_qkv, w_o, bias, sinks)


def _attn_sample_body(x_ref, ck_ref, cv_ref, g_ref, wqkv_ref, bqkv_ref, wo_ref, bias_ref, sink_ref,
                      o_ref, nk_ref, nv_ref, klo_ref, khi_ref, v2_ref):
    x = x_ref[...]
    h = _rmsnorm(x, g_ref[...]).astype(BF16)
    qkv = _dot(h, wqkv_ref[...]) + bqkv_ref[...]
    q = (qkv[:, :Q_DIM] * (HEAD_DIM ** -0.5)).astype(BF16)
    k = qkv[:, Q_DIM:Q_DIM + KV_DIM]
    v = qkv[:, Q_DIM + KV_DIM:]
    ck = ck_ref[...]
    cv = cv_ref[...]
    nk_ref[0:WINDOW - DEC_SEQ, :] = ck[DEC_SEQ:]
    nk_ref[WINDOW - DEC_SEQ:, :] = k
    nv_ref[0:WINDOW - DEC_SEQ, :] = cv[DEC_SEQ:]
    nv_ref[WINDOW - DEC_SEQ:, :] = v
    _split_heads(ck, klo_ref, khi_ref, v2_ref, 0, True)
    _split_heads(k, klo_ref, khi_ref, v2_ref, WINDOW, True)
    _split_heads(cv, klo_ref, khi_ref, v2_ref, 0, False)
    _split_heads(v, klo_ref, khi_ref, v2_ref, WINDOW, False)
    att = _attend_chunk(q, 0, WINDOW + DEC_SEQ, klo_ref, khi_ref, v2_ref, bias_ref, sink_ref, None)
    o_ref[...] = x + _dot(att.astype(BF16), wo_ref[...])


def _attn_sample(x, ck, cv, g, w_qkv, b_qkv, w_o, bias, sinks):
    nkeys = WINDOW + DEC_SEQ
    first = PROMPT_ROWS // DEC_SEQ
    kv_shape = jax.ShapeDtypeStruct((DEC_BATCH, WINDOW, KV_DIM), F32)
    kv_spec = pl.BlockSpec((None, WINDOW, KV_DIM), lambda b: (b, 0, 0))
    return pl.pallas_call(
        _attn_sample_body,
        grid=(DEC_BATCH,),
        in_specs=[
            pl.BlockSpec((DEC_SEQ, D_MODEL), lambda b: (first + b, 0)),
            kv_spec, kv_spec,
            _const_spec((1, D_MODEL)),
            _const_spec((D_MODEL, QKV_DIM)),
            _const_spec((1, QKV_DIM)),
            _const_spec((Q_DIM, D_MODEL)),
            _const_spec((N_KV_HEADS, GROUP * DEC_SEQ, nkeys)),
            _const_spec((N_KV_HEADS, GROUP * DEC_SEQ, 1)),
        ],
        out_specs=[
            pl.BlockSpec((DEC_SEQ, D_MODEL), lambda b: (first + b, 0)),
            kv_spec, kv_spec,
        ],
        out_shape=[jax.ShapeDtypeStruct((ROWS, D_MODEL), F32), kv_shape, kv_shape],
        scratch_shapes=[
            pltpu.VMEM((N_KV_HEADS, nkeys, LANES), BF16),
            pltpu.VMEM((N_KV_HEADS, nkeys, LANES), BF16),
            pltpu.VMEM((N_KV_HEADS, nkeys, 2 * LANES), BF16),
        ],
        input_output_aliases={0: 0},
        compiler_params=pltpu.CompilerParams(
            dimension_semantics=("arbitrary",), vmem_limit_bytes=VMEM_LIMIT),
        name="attn_sample",
    )(x, ck, cv, g, w_qkv, b_qkv, w_o, bias, sinks)


def _conv_core(x, g_ref, win_ref, bin_ref, wdw_ref, bdw_ref, lng_ref, lnb_ref, wout_ref, bout_ref,
               st_ref, ubuf_ref, conv_ref, y_ref):
    rows = x.shape[0]
    h = _rmsnorm(x, g_ref[...]).astype(BF16)
    ag = _dot(h, win_ref[...]) + bin_ref[...]
    u = ag[:, :D_MODEL] * jax.nn.sigmoid(ag[:, D_MODEL:])
    st_ref[...] = u[rows - HIST:]
    for c in range(D_MODEL // LANES):
        ubuf_ref[c, HIST:HIST + rows, :] = u[:, c * LANES:(c + 1) * LANES]

    rb = min(CONV_ROWS, rows)
    for r0 in range(0, rows, rb):
        for c in range(D_MODEL // LANES):
            cols = slice(c * LANES, (c + 1) * LANES)
            acc = jnp.broadcast_to(bdw_ref[:, cols], (rb, LANES))
            for w in range(CONV_WIDTH):
                tap0 = r0 + HIST - CONV_PAD + w
                acc = acc + ubuf_ref[c, tap0:tap0 + rb, :] * wdw_ref[w:w + 1, cols]
            conv_ref[r0:r0 + rb, cols] = acc
        yc = conv_ref[r0:r0 + rb, :]
        xc = yc - jnp.mean(yc, axis=-1, keepdims=True)
        yn = xc * lax.rsqrt(jnp.mean(xc * xc, axis=-1, keepdims=True) + NORM_EPS)
        yn = yn * lng_ref[...] + lnb_ref[...]
        y_ref[r0:r0 + rb, :] = (yn * jax.nn.sigmoid(yn)).astype(BF16)
    return x + _dot(y_ref[...], wout_ref[...]) + bout_ref[...]


def _conv_prompt_body(x_ref, g_ref, win_ref, bin_ref, wdw_ref, bdw_ref, lng_ref, lnb_ref, wout_ref,
                      bout_ref, o_ref, st_ref, ubuf_ref, conv_ref, y_ref):
    @pl.when(pl.program_id(1) == 0)
    def _():
        ubuf_ref[:, 0:HIST, :] = jnp.zeros((D_MODEL // LANES, HIST, LANES), F32)

    o_ref[...] = _conv_core(x_ref[...], g_ref, win_ref, bin_ref, wdw_ref, bdw_ref, lng_ref, lnb_ref,
                            wout_ref, bout_ref, st_ref, ubuf_ref, conv_ref, y_ref)
    ubuf_ref[:, 0:HIST, :] = ubuf_ref[:, MIX_TILE:MIX_TILE + HIST, :]


def _conv_sample_body(x_ref, hist_ref, g_ref, win_ref, bin_ref, wdw_ref, bdw_ref, lng_ref, lnb_ref,
                      wout_ref, bout_ref, o_ref, st_ref, ubuf_ref, conv_ref, y_ref):
    hist = hist_ref[...]
    for c in range(D_MODEL // LANES):
        ubuf_ref[c, 0:HIST, :] = hist[:, c * LANES:(c + 1) * LANES]
    o_ref[...] = _conv_core(x_ref[...], g_ref, win_ref, bin_ref, wdw_ref, bdw_ref, lng_ref, lnb_ref,
                            wout_ref, bout_ref, st_ref, ubuf_ref, conv_ref, y_ref)


def _conv_weight_specs():
    return [
        _const_spec((1, D_MODEL)),
        _const_spec((D_MODEL, 2 * D_MODEL)),
        _const_spec((1, 2 * D_MODEL)),
        _const_spec((CONV_WIDTH, D_MODEL)),
        _const_spec((1, D_MODEL)),
        _const_spec((1, D_MODEL)),
        _const_spec((1, D_MODEL)),
        _const_spec((D_MODEL, D_MODEL)),
        _const_spec((1, D_MODEL)),
    ]


def _conv_prompt(x, weights):
    tiles = SEQ // MIX_TILE
    return pl.pallas_call(
        _conv_prompt_body,
        grid=(BATCH, tiles),
        in_specs=[pl.BlockSpec((MIX_TILE, D_MODEL), lambda b, t: (b * tiles + t, 0))] + _conv_weight_specs(),
        out_specs=[
            pl.BlockSpec((MIX_TILE, D_MODEL), lambda b, t: (b * tiles + t, 0)),
            pl.BlockSpec((None, HIST, D_MODEL), lambda b, t: (b, 0, 0)),
        ],
        out_shape=[jax.ShapeDtypeStruct((ROWS, D_MODEL), F32),
                   jax.ShapeDtypeStruct((BATCH, HIST, D_MODEL), F32)],
        scratch_shapes=[
            pltpu.VMEM((D_MODEL // LANES, HIST + MIX_TILE, LANES), F32),
            pltpu.VMEM((MIX_TILE, D_MODEL), F32),
            pltpu.VMEM((MIX_TILE, D_MODEL), BF16),
        ],
        input_output_aliases={0: 0},
        compiler_params=pltpu.CompilerParams(
            dimension_semantics=("arbitrary", "arbitrary"), vmem_limit_bytes=VMEM_LIMIT),
        name="conv_prompt",
    )(x, *weights)


def _conv_sample(x, hist, weights):
    first = PROMPT_ROWS // DEC_SEQ
    return pl.pallas_call(
        _conv_sample_body,
        grid=(DEC_BATCH,),
        in_specs=[pl.BlockSpec((DEC_SEQ, D_MODEL), lambda b: (first + b, 0)),
                  pl.BlockSpec((None, HIST, D_MODEL), lambda b: (b, 0, 0))] + _conv_weight_specs(),
        out_specs=[
            pl.BlockSpec((DEC_SEQ, D_MODEL), lambda b: (first + b, 0)),
            pl.BlockSpec((None, HIST, D_MODEL), lambda b: (b, 0, 0)),
        ],
        out_shape=[jax.ShapeDtypeStruct((ROWS, D_MODEL), F32),
                   jax.ShapeDtypeStruct((DEC_BATCH, HIST, D_MODEL), F32)],
        scratch_shapes=[
            pltpu.VMEM((D_MODEL // LANES, HIST + DEC_SEQ, LANES), F32),
            pltpu.VMEM((DEC_SEQ, D_MODEL), F32),
            pltpu.VMEM((DEC_SEQ, D_MODEL), BF16),
        ],
        input_output_aliases={0: 0},
        compiler_params=pltpu.CompilerParams(
            dimension_semantics=("arbitrary",), vmem_limit_bytes=VMEM_LIMIT),
        name="conv_sample",
    )(x, hist, *weights)


def _alibi_bias(tq, tk):
    slopes = np.array([2.0 ** (-8.0 * (h + 1) / N_HEADS) for h in range(N_HEADS)], dtype=np.float32)
    i = np.arange(tq)[:, None]
    j = np.arange(tk)[None, :]
    dist = np.abs(i - (j - WINDOW)).astype(np.float32)
    bias = (-slopes[:, None, None] * dist).reshape(N_KV_HEADS, GROUP, tq, tk)
    return jnp.asarray(bias[:, HEAD_ORDER].reshape(N_KV_HEADS, GROUP * tq, tk))


def _sink_rows(sinks, tq):
    s = sinks.astype(F32).reshape(N_KV_HEADS, GROUP)[:, jnp.asarray(HEAD_ORDER)]
    return jnp.repeat(s, tq, axis=1)[:, :, None]


def kernel(x_prompt, x_sample, cache_k, cache_v, state_conv, norm_ffn1, norm_mix, norm_ffn2, norm_final,
           ffn1_w_in, ffn1_w_out, ffn2_w_in, ffn2_w_out, attn_w_qkv, attn_b_qkv, attn_w_o, attn_sinks,
           conv_w_in, conv_b_in, conv_w_dw, conv_b_dw, conv_ln_g, conv_ln_b, conv_w_out, conv_b_out):
    row = lambda a: a.astype(F32).reshape(1, -1)
    x = jnp.concatenate([x_prompt.reshape(PROMPT_ROWS, D_MODEL), x_sample.reshape(SAMPLE_ROWS, D_MODEL)], axis=0)
    bias_p = _alibi_bias(CHUNK, BAND)
    bias_s = _alibi_bias(DEC_SEQ, WINDOW + DEC_SEQ)
    g_final = row(norm_final)
    hist = jnp.pad(state_conv, ((0, 0), (0, 0), (HIST - CONV_PAD, 0), (0, 0)))

    new_k_p, new_v_p, new_k_s, new_v_s, new_c_p, new_c_s = [], [], [], [], [], []
    for layer in range(DEPTH):
        x = _ffn(x, row(norm_ffn1[layer]), ffn1_w_in[layer].astype(BF16), ffn1_w_out[layer].astype(BF16),
                 g_final, False)
        g_mix = row(norm_mix[layer])
        if layer % 2 == 0:
            a = layer // 2
            w_qkv = attn_w_qkv[a].astype(BF16)
            b_qkv = row(attn_b_qkv[a])
            w_o = attn_w_o[a].astype(BF16)
            x, nk, nv = _attn_prompt(x, g_mix, w_qkv, b_qkv, w_o, bias_p, _sink_rows(attn_sinks[a], CHUNK))
            new_k_p.append(nk)
            new_v_p.append(nv)
            x, nk, nv = _attn_sample(x, cache_k[a].reshape(DEC_BATCH, WINDOW, KV_DIM),
                                     cache_v[a].reshape(DEC_BATCH, WINDOW, KV_DIM), g_mix, w_qkv, b_qkv, w_o,
                                     bias_s, _sink_rows(attn_sinks[a], DEC_SEQ))
            new_k_s.append(nk)
            new_v_s.append(nv)
        else:
            c = layer // 2
            weights = (g_mix, conv_w_in[c].astype(BF16), row(conv_b_in[c]), conv_w_dw[c], row(conv_b_dw[c]),
                       row(conv_ln_g[c]), row(conv_ln_b[c]), conv_w_out[c].astype(BF16), row(conv_b_out[c]))
            x, st = _conv_prompt(x, weights)
            new_c_p.append(st[:, HIST - CONV_PAD:])
            x, st = _conv_sample(x, hist[c], weights)
            new_c_s.append(st[:, HIST - CONV_PAD:])
        x = _ffn(x, row(norm_ffn2[layer]), ffn2_w_in[layer].astype(BF16), ffn2_w_out[layer].astype(BF16),
                 g_final, layer == DEPTH - 1)

    kv5 = lambda parts, b: jnp.stack(parts).reshape(len(parts), b, WINDOW, N_KV_HEADS, HEAD_DIM)
    return (x[:PROMPT_ROWS].reshape(BATCH, SEQ, D_MODEL),
            x[PROMPT_ROWS:].reshape(DEC_BATCH, DEC_SEQ, D_MODEL),
            kv5(new_k_p, BATCH), kv5(new_v_p, BATCH), jnp.stack(new_c_p),
            kv5(new_k_s, DEC_BATCH), kv5(new_v_s, DEC_BATCH), jnp.stack(new_c_s))
```

```python
import functools

import numpy as np
import jax
import jax.numpy as jnp
from jax import lax
from jax.experimental import pallas as pl
from jax.experimental.pallas import tpu as pltpu

D_MODEL = 1024
BATCH = 16
SEQ = 2048
DEPTH = 4
DEC_BATCH = 16
DEC_SEQ = 32
CHUNK = 64
N_HEADS = 16
N_KV_HEADS = 4
HEAD_DIM = 64
GROUP = N_HEADS // N_KV_HEADS
Q_DIM = N_HEADS * HEAD_DIM
KV_DIM = N_KV_HEADS * HEAD_DIM
QKV_DIM = Q_DIM + 2 * KV_DIM
WINDOW = 128
BAND = WINDOW + CHUNK
CONV_WIDTH = 31
CONV_PAD = CONV_WIDTH - 1
FFN_DIM = 2816
NORM_EPS = 1e-5
NEG_INF = -1e30

F32 = jnp.float32
BF16 = jnp.bfloat16

LANES = 128
PROMPT_ROWS = BATCH * SEQ
SAMPLE_ROWS = DEC_BATCH * DEC_SEQ
ROWS = PROMPT_ROWS + SAMPLE_ROWS
FFN_TILE = 512
FFN_CHUNK = 256
MIX_TILE = 512
HIST = 32
ATTN_LAG = 1
CONV_ROWS = 64
VMEM_LIMIT = 56 * 1024 * 1024

assert ROWS % FFN_TILE == 0 and SEQ % MIX_TILE == 0 and MIX_TILE % CHUNK == 0
assert FFN_DIM % FFN_CHUNK == 0 and HIST >= CONV_PAD and DEC_SEQ == HIST


def _rmsnorm(x, g):
    return x * lax.rsqrt(jnp.mean(x * x, axis=-1, keepdims=True) + NORM_EPS) * g


def _dot(a, b):
    return jnp.dot(a, b, preferred_element_type=F32)


def _dot_nt(a, b):
    return lax.dot_general(a, b, (((1,), (1,)), ((), ())), preferred_element_type=F32)


def _const_spec(shape):
    return pl.BlockSpec(shape, lambda *_: (0,) * len(shape), pipeline_mode=pl.Buffered(1))


PROMPT_TILES = PROMPT_ROWS // FFN_TILE


def _ffn_body(*refs, first, last):
    refs = list(refs)
    i = pl.program_id(0)
    act_ref = refs.pop()
    if first:
        xp_ref, xs_ref = refs.pop(0), refs.pop(0)
        x = jnp.where(i < PROMPT_TILES, xp_ref[...], xs_ref[...])
    else:
        x = refs.pop(0)[...]
    g_ref, win_ref, wout_ref, gfin_ref, *out_refs = refs
    h = _rmsnorm(x, g_ref[...]).astype(BF16)
    for c in range(FFN_DIM // FFN_CHUNK):
        lo = c * FFN_CHUNK
        gate = _dot(h, win_ref[:, lo:lo + FFN_CHUNK])
        up = _dot(h, win_ref[:, FFN_DIM + lo:FFN_DIM + lo + FFN_CHUNK])
        act_ref[:, lo:lo + FFN_CHUNK] = (gate * jax.nn.sigmoid(gate) * up).astype(BF16)
    y = x + 0.5 * _dot(act_ref[...], wout_ref[...])
    if last:
        y = _rmsnorm(y, gfin_ref[...])
        yp_ref, ys_ref = out_refs

        @pl.when(i < PROMPT_TILES)
        def _():
            yp_ref[...] = y

        @pl.when(i >= PROMPT_TILES)
        def _():
            ys_ref[...] = y
    else:
        out_refs[0][...] = y


def _ffn(xs, g, w_in, w_out, g_final, first=False, last=False):
    tile = (FFN_TILE, D_MODEL)
    prompt_spec = pl.BlockSpec(tile, lambda i: (jnp.minimum(i, PROMPT_TILES - 1), 0))
    sample_spec = pl.BlockSpec(tile, lambda i: (jnp.maximum(i - PROMPT_TILES, 0), 0))
    joint_spec = pl.BlockSpec(tile, lambda i: (i, 0))
    split_shape = [jax.ShapeDtypeStruct((PROMPT_ROWS, D_MODEL), F32),
                   jax.ShapeDtypeStruct((SAMPLE_ROWS, D_MODEL), F32)]
    return pl.pallas_call(
        functools.partial(_ffn_body, first=first, last=last),
        grid=(ROWS // FFN_TILE,),
        in_specs=([prompt_spec, sample_spec] if first else [joint_spec]) + [
            _const_spec((1, D_MODEL)),
            _const_spec((D_MODEL, 2 * FFN_DIM)),
            _const_spec((FFN_DIM, D_MODEL)),
            _const_spec((1, D_MODEL)),
        ],
        out_specs=[prompt_spec, sample_spec] if last else joint_spec,
        out_shape=split_shape if last else jax.ShapeDtypeStruct((ROWS, D_MODEL), F32),
        scratch_shapes=[pltpu.VMEM((FFN_TILE, FFN_DIM), BF16)],
        compiler_params=pltpu.CompilerParams(
            dimension_semantics=("arbitrary",), vmem_limit_bytes=VMEM_LIMIT),
        name="ffn_first" if first else "ffn_last" if last else "ffn",
    )(*xs, g, w_in, w_out, g_final)


def _stage_q(q, qz_ref, chunk):
    rows = q.shape[0]
    lane = lax.broadcasted_iota(jnp.int32, (rows, LANES), 1)
    for h in range(N_HEADS):
        kh, g = divmod(h, GROUP)
        blk = q[:, (h // 2) * LANES:(h // 2 + 1) * LANES]
        if h % 2 != kh % 2:
            blk = pltpu.roll(blk, HEAD_DIM, 1)
        keep = (lane < HEAD_DIM) if kh % 2 == 0 else (lane >= HEAD_DIM)
        blk = jnp.where(keep, blk, 0.0).astype(BF16)
        for c in range(rows // chunk):
            qz_ref[kh, c, g * chunk:(g + 1) * chunk, :] = blk[c * chunk:(c + 1) * chunk]


def _stage_kv(k, v, kp_ref, v2_ref, row0):
    rows = k.shape[0]
    lane = lax.broadcasted_iota(jnp.int32, (rows, LANES), 1)
    for p in range(KV_DIM // LANES):
        cols = slice(p * LANES, (p + 1) * LANES)
        kp_ref[p, row0:row0 + rows, :] = k[:, cols].astype(BF16)
        pair = v[:, cols]
        lo_even = jnp.where(lane < HEAD_DIM, pair, 0.0)
        hi_odd = jnp.where(lane >= HEAD_DIM, pair, 0.0)
        hi_even = pltpu.roll(lo_even, HEAD_DIM, 1)
        lo_odd = pltpu.roll(hi_odd, HEAD_DIM, 1)
        for kh, lo, hi in ((2 * p, lo_even, hi_even), (2 * p + 1, lo_odd, hi_odd)):
            v2_ref[kh, row0:row0 + rows, 0:LANES] = lo.astype(BF16)
            v2_ref[kh, row0:row0 + rows, LANES:2 * LANES] = hi.astype(BF16)


def _scores(qz, kband, bias, sink, valid):
    s = _dot_nt(qz, kband) + bias
    if valid is not None:
        s = jnp.where(valid, s, NEG_INF)
    return s, jnp.maximum(jnp.max(s, axis=-1, keepdims=True), sink)


def _probs(s, m, sink):
    return jnp.exp(s - m).astype(BF16), jnp.exp(sink - m)


def _weighted(p, sink_p, vband):
    t = p.shape[0] // GROUP
    o2 = _dot(p, vband)
    inv = 1.0 / (_dot(p, jnp.ones((p.shape[1], LANES), BF16)) + sink_p)
    o2 = o2 * jnp.concatenate([inv, inv], axis=1)
    return (o2[0:t, 0:LANES] + o2[t:2 * t, LANES:2 * LANES],
            o2[2 * t:3 * t, 0:LANES] + o2[3 * t:4 * t, LANES:2 * LANES])


PAIR = 2 * CHUNK
PAIR_BAND = WINDOW + PAIR
ONES_ROWS = 16


def _attn_prompt_body(x_ref, g_ref, wqkv_ref, bqkv_ref, wo_ref, bias_ref, sink_ref,
                      o_ref, nk_ref, nv_ref, qz_ref, kp_ref, vt_ref, att_ref):
    t = pl.program_id(1)

    @pl.when(t == 0)
    def _():
        kp_ref[:, 0:WINDOW, :] = jnp.zeros((KV_DIM // LANES, WINDOW, LANES), BF16)
        vt_ref[:, 0:HEAD_DIM, 0:WINDOW] = jnp.zeros((N_KV_HEADS, HEAD_DIM, WINDOW), BF16)
        vt_ref[:, HEAD_DIM:, 0:WINDOW] = jnp.ones((N_KV_HEADS, ONES_ROWS, WINDOW), BF16)

    x = x_ref[...]
    h = _rmsnorm(x, g_ref[...]).astype(BF16)
    qkv = _dot(h, wqkv_ref[...]) + bqkv_ref[...]
    k = qkv[:, Q_DIM:Q_DIM + KV_DIM]
    v = qkv[:, Q_DIM + KV_DIM:]
    nk_ref[...] = k[MIX_TILE - WINDOW:]
    nv_ref[...] = v[MIX_TILE - WINDOW:]
    _stage_q(qkv[:, :Q_DIM] * (HEAD_DIM ** -0.5), qz_ref, PAIR)
    for p in range(KV_DIM // LANES):
        kp_ref[p, WINDOW:, :] = k[:, p * LANES:(p + 1) * LANES].astype(BF16)
    v_t = v.T
    for kh in range(N_KV_HEADS):
        vt_ref[kh, 0:HEAD_DIM, WINDOW:] = v_t[kh * HEAD_DIM:(kh + 1) * HEAD_DIM, :].astype(BF16)
        vt_ref[kh, HEAD_DIM:, WINDOW:] = jnp.ones((ONES_ROWS, MIX_TILE), BF16)

    key_row = lax.broadcasted_iota(jnp.int32, (PAIR_BAND, GROUP * PAIR), 0)
    units = [(pp, kh) for pp in range(MIX_TILE // PAIR) for kh in range(N_KV_HEADS)]
    keys = lambda pp: slice(pp * PAIR, pp * PAIR + PAIR_BAND)

    def scores(pp, kh):
        s = _dot_nt(kp_ref[kh // 2, keys(pp), :], qz_ref[kh, pp]) + bias_ref[kh]
        if pp == 0:
            s = jnp.where(key_row >= WINDOW - t * MIX_TILE, s, NEG_INF)
        return s, jnp.maximum(jnp.max(s, axis=0, keepdims=True), sink_ref[kh])

    def finish(pp, kh, p, sink_p):
        o = _dot(vt_ref[kh, :, keys(pp)], p)
        o = o[0:HEAD_DIM, :] * (1.0 / (o[HEAD_DIM:HEAD_DIM + 1, :] + sink_p))
        for g in range(GROUP):
            head = kh * GROUP + g
            att_ref[head * HEAD_DIM:(head + 1) * HEAD_DIM, pp * PAIR:(pp + 1) * PAIR] = (
                o[:, g * PAIR:(g + 1) * PAIR].astype(BF16))

    scored, exped = {}, {}
    for step in range(len(units) + 2 * ATTN_LAG):
        if step < len(units):
            scored[step] = scores(*units[step])
        u = step - ATTN_LAG
        if 0 <= u < len(units):
            s, m = scored.pop(u)
            exped[u] = _probs(s, m, sink_ref[units[u][1]])
        u = step - 2 * ATTN_LAG
        if u >= 0:
            finish(*units[u], *exped.pop(u))

    out = lax.dot_general(att_ref[...], wo_ref[...], (((0,), (0,)), ((), ())), preferred_element_type=F32)
    o_ref[...] = x + out
    kp_ref[:, 0:WINDOW, :] = kp_ref[:, MIX_TILE:MIX_TILE + WINDOW, :]
    vt_ref[:, :, 0:WINDOW] = vt_ref[:, :, MIX_TILE:MIX_TILE + WINDOW]


def _attn_prompt(x, g, w_qkv, b_qkv, w_o, bias, sinks):
    tiles = SEQ // MIX_TILE
    kv_shape = jax.ShapeDtypeStruct((BATCH, WINDOW, KV_DIM), F32)
    kv_spec = pl.BlockSpec((None, WINDOW, KV_DIM), lambda b, t: (b, 0, 0))
    return pl.pallas_call(
        _attn_prompt_body,
        grid=(BATCH, tiles),
        in_specs=[
            pl.BlockSpec((MIX_TILE, D_MODEL), lambda b, t: (b * tiles + t, 0)),
            _const_spec((1, D_MODEL)),
            _const_spec((D_MODEL, QKV_DIM)),
            _const_spec((1, QKV_DIM)),
            _const_spec((Q_DIM, D_MODEL)),
            _const_spec((N_KV_HEADS, PAIR_BAND, GROUP * PAIR)),
            _const_spec((N_KV_HEADS, 1, GROUP * PAIR)),
        ],
        out_specs=[
            pl.BlockSpec((MIX_TILE, D_MODEL), lambda b, t: (b * tiles + t, 0)),
            kv_spec, kv_spec,
        ],
        out_shape=[jax.ShapeDtypeStruct((ROWS, D_MODEL), F32), kv_shape, kv_shape],
        scratch_shapes=[
            pltpu.VMEM((N_KV_HEADS, MIX_TILE // PAIR, GROUP * PAIR, LANES), BF16),
            pltpu.VMEM((KV_DIM // LANES, WINDOW + MIX_TILE, LANES), BF16),
            pltpu.VMEM((N_KV_HEADS, HEAD_DIM + ONES_ROWS, WINDOW + MIX_TILE), BF16),
            pltpu.VMEM((Q_DIM, MIX_TILE), BF16),
        ],
        input_output_aliases={0: 0},
        compiler_params=pltpu.CompilerParams(
            dimension_semantics=("arbitrary", "arbitrary"), vmem_limit_bytes=VMEM_LIMIT),
        name="attn_prompt",
    )(x, g, w_qkv, b_qkv, w_o, bias, sinks)


def _attn_sample_body(x_ref, ck_ref, cv_ref, g_ref, wqkv_ref, bqkv_ref, wo_ref, bias_ref, sink_ref,
                      o_ref, nk_ref, nv_ref, qz_ref, kp_ref, v2_ref):
    x = x_ref[...]
    h = _rmsnorm(x, g_ref[...]).astype(BF16)
    qkv = _dot(h, wqkv_ref[...]) + bqkv_ref[...]
    k = qkv[:, Q_DIM:Q_DIM + KV_DIM]
    v = qkv[:, Q_DIM + KV_DIM:]
    ck = ck_ref[...]
    cv = cv_ref[...]
    nk_ref[0:WINDOW - DEC_SEQ, :] = ck[DEC_SEQ:]
    nk_ref[WINDOW - DEC_SEQ:, :] = k
    nv_ref[0:WINDOW - DEC_SEQ, :] = cv[DEC_SEQ:]
    nv_ref[WINDOW - DEC_SEQ:, :] = v
    _stage_q(qkv[:, :Q_DIM] * (HEAD_DIM ** -0.5), qz_ref, DEC_SEQ)
    _stage_kv(ck, cv, kp_ref, v2_ref, 0)
    _stage_kv(k, v, kp_ref, v2_ref, WINDOW)
    pairs = []
    for kh in range(N_KV_HEADS):
        s, m = _scores(qz_ref[kh, 0], kp_ref[kh // 2], bias_ref[kh], sink_ref[kh], None)
        pairs.extend(_weighted(*_probs(s, m, sink_ref[kh]), v2_ref[kh]))
    att = jnp.concatenate(pairs, axis=1).astype(BF16)
    o_ref[...] = x + _dot(att, wo_ref[...])


def _attn_sample(x, ck, cv, g, w_qkv, b_qkv, w_o, bias, sinks):
    nkeys = WINDOW + DEC_SEQ
    first = PROMPT_ROWS // DEC_SEQ
    kv_shape = jax.ShapeDtypeStruct((DEC_BATCH, WINDOW, KV_DIM), F32)
    kv_spec = pl.BlockSpec((None, WINDOW, KV_DIM), lambda b: (b, 0, 0))
    return pl.pallas_call(
        _attn_sample_body,
        grid=(DEC_BATCH,),
        in_specs=[
            pl.BlockSpec((DEC_SEQ, D_MODEL), lambda b: (first + b, 0)),
            kv_spec, kv_spec,
            _const_spec((1, D_MODEL)),
            _const_spec((D_MODEL, QKV_DIM)),
            _const_spec((1, QKV_DIM)),
            _const_spec((Q_DIM, D_MODEL)),
            _const_spec((N_KV_HEADS, GROUP * DEC_SEQ, nkeys)),
            _const_spec((N_KV_HEADS, GROUP * DEC_SEQ, 1)),
        ],
        out_specs=[
            pl.BlockSpec((DEC_SEQ, D_MODEL), lambda b: (first + b, 0)),
            kv_spec, kv_spec,
        ],
        out_shape=[jax.ShapeDtypeStruct((ROWS, D_MODEL), F32), kv_shape, kv_shape],
        scratch_shapes=[
            pltpu.VMEM((N_KV_HEADS, 1, GROUP * DEC_SEQ, LANES), BF16),
            pltpu.VMEM((KV_DIM // LANES, nkeys, LANES), BF16),
            pltpu.VMEM((N_KV_HEADS, nkeys, 2 * LANES), BF16),
        ],
        input_output_aliases={0: 0},
        compiler_params=pltpu.CompilerParams(
            dimension_semantics=("arbitrary",), vmem_limit_bytes=VMEM_LIMIT),
        name="attn_sample",
    )(x, ck, cv, g, w_qkv, b_qkv, w_o, bias, sinks)


def _conv_core(x, g_ref, win_ref, bin_ref, wdw_ref, bdw_ref, lng_ref, lnb_ref, wout_ref, bout_ref,
               st_ref, ubuf_ref, conv_ref, y_ref):
    rows = x.shape[0]
    h = _rmsnorm(x, g_ref[...]).astype(BF16)
    ag = _dot(h, win_ref[...]) + bin_ref[...]
    u = ag[:, :D_MODEL] * jax.nn.sigmoid(ag[:, D_MODEL:])
    st_ref[...] = u[rows - HIST:]
    for c in range(D_MODEL // LANES):
        ubuf_ref[c, HIST:HIST + rows, :] = u[:, c * LANES:(c + 1) * LANES]

    rb = min(CONV_ROWS, rows)
    for r0 in range(0, rows, rb):
        for c in range(D_MODEL // LANES):
            cols = slice(c * LANES, (c + 1) * LANES)
            acc = jnp.broadcast_to(bdw_ref[:, cols], (rb, LANES))
            for w in range(CONV_WIDTH):
                tap0 = r0 + HIST - CONV_PAD + w
                acc = acc + ubuf_ref[c, tap0:tap0 + rb, :] * wdw_ref[w:w + 1, cols]
            conv_ref[r0:r0 + rb, cols] = acc
        yc = conv_ref[r0:r0 + rb, :]
        xc = yc - jnp.mean(yc, axis=-1, keepdims=True)
        yn = xc * lax.rsqrt(jnp.mean(xc * xc, axis=-1, keepdims=True) + NORM_EPS)
        yn = yn * lng_ref[...] + lnb_ref[...]
        y_ref[r0:r0 + rb, :] = (yn * jax.nn.sigmoid(yn)).astype(BF16)
    return x + _dot(y_ref[...], wout_ref[...]) + bout_ref[...]


def _conv_prompt_body(x_ref, g_ref, win_ref, bin_ref, wdw_ref, bdw_ref, lng_ref, lnb_ref, wout_ref,
                      bout_ref, o_ref, st_ref, ubuf_ref, conv_ref, y_ref):
    @pl.when(pl.program_id(1) == 0)
    def _():
        ubuf_ref[:, 0:HIST, :] = jnp.zeros((D_MODEL // LANES, HIST, LANES), F32)

    o_ref[...] = _conv_core(x_ref[...], g_ref, win_ref, bin_ref, wdw_ref, bdw_ref, lng_ref, lnb_ref,
                            wout_ref, bout_ref, st_ref, ubuf_ref, conv_ref, y_ref)
    ubuf_ref[:, 0:HIST, :] = ubuf_ref[:, MIX_TILE:MIX_TILE + HIST, :]


def _conv_sample_body(x_ref, hist_ref, g_ref, win_ref, bin_ref, wdw_ref, bdw_ref, lng_ref, lnb_ref,
                      wout_ref, bout_ref, o_ref, st_ref, ubuf_ref, conv_ref, y_ref):
    hist = hist_ref[...]
    for c in range(D_MODEL // LANES):
        ubuf_ref[c, 0:HIST, :] = hist[:, c * LANES:(c + 1) * LANES]
    o_ref[...] = _conv_core(x_ref[...], g_ref, win_ref, bin_ref, wdw_ref, bdw_ref, lng_ref, lnb_ref,
                            wout_ref, bout_ref, st_ref, ubuf_ref, conv_ref, y_ref)


def _conv_weight_specs():
    return [
        _const_spec((1, D_MODEL)),
        _const_spec((D_MODEL, 2 * D_MODEL)),
        _const_spec((1, 2 * D_MODEL)),
        _const_spec((CONV_WIDTH, D_MODEL)),
        _const_spec((1, D_MODEL)),
        _const_spec((1, D_MODEL)),
        _const_spec((1, D_MODEL)),
        _const_spec((D_MODEL, D_MODEL)),
        _const_spec((1, D_MODEL)),
    ]


def _conv_prompt(x, weights):
    tiles = SEQ // MIX_TILE
    return pl.pallas_call(
        _conv_prompt_body,
        grid=(BATCH, tiles),
        in_specs=[pl.BlockSpec((MIX_TILE, D_MODEL), lambda b, t: (b * tiles + t, 0))] + _conv_weight_specs(),
        out_specs=[
            pl.BlockSpec((MIX_TILE, D_MODEL), lambda b, t: (b * tiles + t, 0)),
            pl.BlockSpec((None, HIST, D_MODEL), lambda b, t: (b, 0, 0)),
        ],
        out_shape=[jax.ShapeDtypeStruct((ROWS, D_MODEL), F32),
                   jax.ShapeDtypeStruct((BATCH, HIST, D_MODEL), F32)],
        scratch_shapes=[
            pltpu.VMEM((D_MODEL // LANES, HIST + MIX_TILE, LANES), F32),
            pltpu.VMEM((MIX_TILE, D_MODEL), F32),
            pltpu.VMEM((MIX_TILE, D_MODEL), BF16),
        ],
        input_output_aliases={0: 0},
        compiler_params=pltpu.CompilerParams(
            dimension_semantics=("arbitrary", "arbitrary"), vmem_limit_bytes=VMEM_LIMIT),
        name="conv_prompt",
    )(x, *weights)


def _conv_sample(x, hist, weights):
    first = PROMPT_ROWS // DEC_SEQ
    return pl.pallas_call(
        _conv_sample_body,
        grid=(DEC_BATCH,),
        in_specs=[pl.BlockSpec((DEC_SEQ, D_MODEL), lambda b: (first + b, 0)),
                  pl.BlockSpec((None, HIST, D_MODEL), lambda b: (b, 0, 0))] + _conv_weight_specs(),
        out_specs=[
            pl.BlockSpec((DEC_SEQ, D_MODEL), lambda b: (first + b, 0)),
            pl.BlockSpec((None, HIST, D_MODEL), lambda b: (b, 0, 0)),
        ],
        out_shape=[jax.ShapeDtypeStruct((ROWS, D_MODEL), F32),
                   jax.ShapeDtypeStruct((DEC_BATCH, HIST, D_MODEL), F32)],
        scratch_shapes=[
            pltpu.VMEM((D_MODEL // LANES, HIST + DEC_SEQ, LANES), F32),
            pltpu.VMEM((DEC_SEQ, D_MODEL), F32),
            pltpu.VMEM((DEC_SEQ, D_MODEL), BF16),
        ],
        input_output_aliases={0: 0},
        compiler_params=pltpu.CompilerParams(
            dimension_semantics=("arbitrary",), vmem_limit_bytes=VMEM_LIMIT),
        name="conv_sample",
    )(x, hist, *weights)


def _alibi_bias(tq, tk):
    slopes = np.array([2.0 ** (-8.0 * (h + 1) / N_HEADS) for h in range(N_HEADS)], dtype=np.float32)
    i = np.arange(tq)[:, None]
    j = np.arange(tk)[None, :]
    dist = np.abs(i - (j - WINDOW)).astype(np.float32)
    return jnp.asarray((-slopes[:, None, None] * dist).reshape(N_KV_HEADS, GROUP * tq, tk))


def _pair_bias():
    slopes = np.array([2.0 ** (-8.0 * (h + 1) / N_HEADS) for h in range(N_HEADS)], dtype=np.float32)
    r = np.arange(PAIR_BAND)[:, None, None]
    cc = np.arange(PAIR // CHUNK)[None, :, None]
    i = np.arange(CHUNK)[None, None, :]
    j = r - cc * CHUNK
    dist = np.abs(i - (j - WINDOW)).astype(np.float32)
    bias = -slopes[:, None, None, None] * dist[None]
    bias = np.where((j >= 0) & (j < BAND), bias, np.float32(NEG_INF)).astype(np.float32)
    bias = bias.reshape(N_KV_HEADS, GROUP, PAIR_BAND, PAIR).transpose(0, 2, 1, 3)
    return jnp.asarray(bias.reshape(N_KV_HEADS, PAIR_BAND, GROUP * PAIR))


def _sink_lanes(sinks):
    return jnp.repeat(sinks.astype(F32).reshape(N_KV_HEADS, GROUP), PAIR, axis=1)[:, None, :]


def _sink_rows(sinks, tq):
    s = sinks.astype(F32).reshape(N_KV_HEADS, GROUP)
    return jnp.repeat(s, tq, axis=1)[:, :, None]


def kernel(x_prompt, x_sample, cache_k, cache_v, state_conv, norm_ffn1, norm_mix, norm_ffn2, norm_final,
           ffn1_w_in, ffn1_w_out, ffn2_w_in, ffn2_w_out, attn_w_qkv, attn_b_qkv, attn_w_o, attn_sinks,
           conv_w_in, conv_b_in, conv_w_dw, conv_b_dw, conv_ln_g, conv_ln_b, conv_w_out, conv_b_out):
    row = lambda a: a.astype(F32).reshape(1, -1)
    xs = (x_prompt.reshape(PROMPT_ROWS, D_MODEL), x_sample.reshape(SAMPLE_ROWS, D_MODEL))
    bias_p = _pair_bias()
    bias_s = _alibi_bias(DEC_SEQ, WINDOW + DEC_SEQ)
    g_final = row(norm_final)
    hist = jnp.pad(state_conv, ((0, 0), (0, 0), (HIST - CONV_PAD, 0), (0, 0)))

    new_k_p, new_v_p, new_k_s, new_v_s, new_c_p, new_c_s = [], [], [], [], [], []
    for layer in range(DEPTH):
        x = _ffn(xs if layer == 0 else (x,), row(norm_ffn1[layer]), ffn1_w_in[layer].astype(BF16),
                 ffn1_w_out[layer].astype(BF16), g_final, first=layer == 0)
        g_mix = row(norm_mix[layer])
        if layer % 2 == 0:
            a = layer // 2
            w_qkv = attn_w_qkv[a].astype(BF16)
            b_qkv = row(attn_b_qkv[a])
            w_o = attn_w_o[a].astype(BF16)
            x, nk, nv = _attn_prompt(x, g_mix, w_qkv, b_qkv, w_o, bias_p, _sink_lanes(attn_sinks[a]))
            new_k_p.append(nk)
            new_v_p.append(nv)
            x, nk, nv = _attn_sample(x, cache_k[a].reshape(DEC_BATCH, WINDOW, KV_DIM),
                                     cache_v[a].reshape(DEC_BATCH, WINDOW, KV_DIM), g_mix, w_qkv, b_qkv, w_o,
                                     bias_s, _sink_rows(attn_sinks[a], DEC_SEQ))
            new_k_s.append(nk)
            new_v_s.append(nv)
        else:
            c = layer // 2
            weights = (g_mix, conv_w_in[c].astype(BF16), row(conv_b_in[c]), conv_w_dw[c], row(conv_b_dw[c]),
                       row(conv_ln_g[c]), row(conv_ln_b[c]), conv_w_out[c].astype(BF16), row(conv_b_out[c]))
            x, st = _conv_prompt(x, weights)
            new_c_p.append(st[:, HIST - CONV_PAD:])
            x, st = _conv_sample(x, hist[c], weights)
            new_c_s.append(st[:, HIST - CONV_PAD:])
        x = _ffn((x,), row(norm_ffn2[layer]), ffn2_w_in[layer].astype(BF16), ffn2_w_out[layer].astype(BF16),
                 g_final, last=layer == DEPTH - 1)

    y_prompt, y_sample = x
    kv5 = lambda parts, b: jnp.stack(parts).reshape(len(parts), b, WINDOW, N_KV_HEADS, HEAD_DIM)
    return (y_prompt.reshape(BATCH, SEQ, D_MODEL),
            y_sample.reshape(DEC_BATCH, DEC_SEQ, D_MODEL),
            kv5(new_k_p, BATCH), kv5(new_v_p, BATCH), jnp.stack(new_c_p),
            kv5(new_k_s, DEC_BATCH), kv5(new_v_s, DEC_BATCH), jnp.stack(new_c_s))
```

```python
import functools

import numpy as np
import jax
import jax.numpy as jnp
from jax import lax
from jax.experimental import pallas as pl
from jax.experimental.pallas import tpu as pltpu

D_MODEL = 1024
BATCH = 16
SEQ = 2048
DEPTH = 4
DEC_BATCH = 16
DEC_SEQ = 32
CHUNK = 64
N_HEADS = 16
N_KV_HEADS = 4
HEAD_DIM = 64
GROUP = N_HEADS // N_KV_HEADS
Q_DIM = N_HEADS * HEAD_DIM
KV_DIM = N_KV_HEADS * HEAD_DIM
QKV_DIM = Q_DIM + 2 * KV_DIM
WINDOW = 128
BAND = WINDOW + CHUNK
CONV_WIDTH = 31
CONV_PAD = CONV_WIDTH - 1
FFN_DIM = 2816
NORM_EPS = 1e-5
NEG_INF = -1e30
LOG2E = 1.4426950408889634

F32 = jnp.float32
BF16 = jnp.bfloat16

LANES = 128
PROMPT_ROWS = BATCH * SEQ
SAMPLE_ROWS = DEC_BATCH * DEC_SEQ
ROWS = PROMPT_ROWS + SAMPLE_ROWS
FFN_TILE = 512
FFN_CHUNK = 256
MIX_TILE = 1024
PROMPT_TILES = PROMPT_ROWS // FFN_TILE
SAMPLE_BLOCK = PROMPT_ROWS // SAMPLE_ROWS
PAIR = 2 * CHUNK
PAIR_BAND = WINDOW + PAIR
ONES_ROWS = 16
ATTN_LAG = 1
HIST = 32
CONV_ROWS = 64
VMEM_LIMIT = 56 * 1024 * 1024

assert ROWS % FFN_TILE == 0 and SEQ % MIX_TILE == 0 and MIX_TILE % PAIR == 0
assert PROMPT_ROWS % SAMPLE_ROWS == 0 and SAMPLE_ROWS % FFN_TILE == 0
assert FFN_DIM % FFN_CHUNK == 0 and HIST >= CONV_PAD and DEC_SEQ == HIST


def _rmsnorm(x, g):
    return x * lax.rsqrt(jnp.mean(x * x, axis=-1, keepdims=True) + NORM_EPS) * g


def _dot(a, b):
    return jnp.dot(a, b, preferred_element_type=F32)


def _dot_nt(a, b):
    return lax.dot_general(a, b, (((1,), (1,)), ((), ())), preferred_element_type=F32)


def _const_spec(shape):
    return pl.BlockSpec(shape, lambda *_: (0,) * len(shape), pipeline_mode=pl.Buffered(1))


def _layer_spec(shape, layer):
    return pl.BlockSpec((None,) + shape, lambda *_: (layer,) + (0,) * len(shape), pipeline_mode=pl.Buffered(1))


def _params(dimension_semantics):
    return pltpu.CompilerParams(dimension_semantics=dimension_semantics, vmem_limit_bytes=VMEM_LIMIT)


def _ffn_body(*refs, first, last):
    refs = list(refs)
    i = pl.program_id(0)
    act_ref = refs.pop()
    if first:
        xp_ref, xs_ref = refs.pop(0), refs.pop(0)
        x = jnp.where(i < PROMPT_TILES, xp_ref[...], xs_ref[...])
    else:
        x = refs.pop(0)[...]
    g_ref, win_ref, wout_ref, gfin_ref, *out_refs = refs
    h = _rmsnorm(x, g_ref[...]).astype(BF16)
    for c in range(FFN_DIM // FFN_CHUNK):
        lo = c * FFN_CHUNK
        gate = _dot(h, win_ref[:, lo:lo + FFN_CHUNK])
        up = _dot(h, win_ref[:, FFN_DIM + lo:FFN_DIM + lo + FFN_CHUNK])
        act_ref[:, lo:lo + FFN_CHUNK] = (gate * jax.nn.sigmoid(gate) * up).astype(BF16)
    y = x + 0.5 * _dot(act_ref[...], wout_ref[...])
    if last:
        y = _rmsnorm(y, gfin_ref[...])
        yp_ref, ys_ref = out_refs

        @pl.when(i < PROMPT_TILES)
        def _():
            yp_ref[...] = y

        @pl.when(i >= PROMPT_TILES)
        def _():
            ys_ref[...] = y
    else:
        out_refs[0][...] = y


def _ffn(xs, g, w_in, w_out, layer, g_final, first=False, last=False):
    tile = (FFN_TILE, D_MODEL)
    prompt_spec = pl.BlockSpec(tile, lambda i: (jnp.minimum(i, PROMPT_TILES - 1), 0))
    sample_spec = pl.BlockSpec(tile, lambda i: (jnp.maximum(i - PROMPT_TILES, 0), 0))
    joint_spec = pl.BlockSpec(tile, lambda i: (i, 0))
    split_shape = [jax.ShapeDtypeStruct((PROMPT_ROWS, D_MODEL), F32),
                   jax.ShapeDtypeStruct((SAMPLE_ROWS, D_MODEL), F32)]
    return pl.pallas_call(
        functools.partial(_ffn_body, first=first, last=last),
        grid=(ROWS // FFN_TILE,),
        in_specs=([prompt_spec, sample_spec] if first else [joint_spec]) + [
            _layer_spec((1, D_MODEL), layer),
            _layer_spec((D_MODEL, 2 * FFN_DIM), layer),
            _layer_spec((FFN_DIM, D_MODEL), layer),
            _const_spec((1, D_MODEL)),
        ],
        out_specs=[prompt_spec, sample_spec] if last else joint_spec,
        out_shape=split_shape if last else jax.ShapeDtypeStruct((ROWS, D_MODEL), F32),
        scratch_shapes=[pltpu.VMEM((FFN_TILE, FFN_DIM), BF16)],
        compiler_params=_params(("arbitrary",)),
        name="ffn_first" if first else "ffn_last" if last else "ffn",
    )(*xs, g, w_in, w_out, g_final)


def _stage_q(q, qz_ref, chunk):
    rows = q.shape[0]
    lane = lax.broadcasted_iota(jnp.int32, (rows, LANES), 1)
    for h in range(N_HEADS):
        kh, g = divmod(h, GROUP)
        blk = q[:, (h // 2) * LANES:(h // 2 + 1) * LANES]
        if h % 2 != kh % 2:
            blk = pltpu.roll(blk, HEAD_DIM, 1)
        keep = (lane < HEAD_DIM) if kh % 2 == 0 else (lane >= HEAD_DIM)
        blk = jnp.where(keep, blk, 0.0).astype(BF16)
        for c in range(rows // chunk):
            qz_ref[kh, c, g * chunk:(g + 1) * chunk, :] = blk[c * chunk:(c + 1) * chunk]


def _attn_prompt_body(x_ref, g_ref, wqkv_ref, bqkv_ref, wo_ref, bias_ref, sink_ref,
                      o_ref, nk_ref, nv_ref, qz_ref, kp_ref, vt_ref, att_ref):
    t = pl.program_id(1)

    @pl.when(t == 0)
    def _():
        kp_ref[:, 0:WINDOW, :] = jnp.zeros((KV_DIM // LANES, WINDOW, LANES), BF16)
        vt_ref[:, 0:HEAD_DIM, 0:WINDOW] = jnp.zeros((N_KV_HEADS, HEAD_DIM, WINDOW), BF16)
        vt_ref[:, HEAD_DIM:, 0:WINDOW] = jnp.ones((N_KV_HEADS, ONES_ROWS, WINDOW), BF16)

    x = x_ref[...]
    h = _rmsnorm(x, g_ref[...]).astype(BF16)
    qkv = _dot(h, wqkv_ref[...]) + bqkv_ref[...]
    k = qkv[:, Q_DIM:Q_DIM + KV_DIM]
    v = qkv[:, Q_DIM + KV_DIM:]
    nk_ref[...] = k[MIX_TILE - WINDOW:]
    nv_ref[...] = v[MIX_TILE - WINDOW:]
    _stage_q(qkv[:, :Q_DIM] * (LOG2E * HEAD_DIM ** -0.5), qz_ref, PAIR)
    for p in range(KV_DIM // LANES):
        kp_ref[p, WINDOW:, :] = k[:, p * LANES:(p + 1) * LANES].astype(BF16)
    v_t = v.T
    for kh in range(N_KV_HEADS):
        vt_ref[kh, 0:HEAD_DIM, WINDOW:] = v_t[kh * HEAD_DIM:(kh + 1) * HEAD_DIM, :].astype(BF16)
        vt_ref[kh, HEAD_DIM:, WINDOW:] = jnp.ones((ONES_ROWS, MIX_TILE), BF16)

    key_row = lax.broadcasted_iota(jnp.int32, (PAIR_BAND, GROUP * PAIR), 0)
    units = [(pp, kh) for pp in range(MIX_TILE // PAIR) for kh in range(N_KV_HEADS)]
    keys = lambda pp: slice(pp * PAIR, pp * PAIR + PAIR_BAND)

    def scores(pp, kh):
        s = _dot_nt(kp_ref[kh // 2, keys(pp), :], qz_ref[kh, pp]) + bias_ref[kh]
        if pp == 0:
            s = jnp.where(key_row >= WINDOW - t * MIX_TILE, s, NEG_INF)
        return s, jnp.maximum(jnp.max(s, axis=0, keepdims=True), sink_ref[kh])

    def finish(pp, kh, p, sink_p):
        o = _dot(vt_ref[kh, :, keys(pp)], p)
        o = o[0:HEAD_DIM, :] * (1.0 / (o[HEAD_DIM:HEAD_DIM + 1, :] + sink_p))
        for g in range(GROUP):
            head = kh * GROUP + g
            att_ref[head * HEAD_DIM:(head + 1) * HEAD_DIM, pp * PAIR:(pp + 1) * PAIR] = (
                o[:, g * PAIR:(g + 1) * PAIR].astype(BF16))

    scored, exped = {}, {}
    for step in range(len(units) + 2 * ATTN_LAG):
        if step < len(units):
            scored[step] = scores(*units[step])
        u = step - ATTN_LAG
        if 0 <= u < len(units):
            s, m = scored.pop(u)
            exped[u] = jnp.exp2(s - m).astype(BF16), jnp.exp2(sink_ref[units[u][1]] - m)
        u = step - 2 * ATTN_LAG
        if u >= 0:
            finish(*units[u], *exped.pop(u))

    out = lax.dot_general(att_ref[...], wo_ref[...], (((0,), (0,)), ((), ())), preferred_element_type=F32)
    o_ref[...] = x + out
    kp_ref[:, 0:WINDOW, :] = kp_ref[:, MIX_TILE:MIX_TILE + WINDOW, :]
    vt_ref[:, :, 0:WINDOW] = vt_ref[:, :, MIX_TILE:MIX_TILE + WINDOW]


def _attn_prompt(x, g, w_qkv, b_qkv, w_o, bias, sinks, layer, a):
    tiles = SEQ // MIX_TILE
    kv_shape = jax.ShapeDtypeStruct((BATCH, WINDOW, KV_DIM), F32)
    kv_spec = pl.BlockSpec((None, WINDOW, KV_DIM), lambda b, t: (b, 0, 0))
    return pl.pallas_call(
        _attn_prompt_body,
        grid=(BATCH, tiles),
        in_specs=[
            pl.BlockSpec((MIX_TILE, D_MODEL), lambda b, t: (b * tiles + t, 0)),
            _layer_spec((1, D_MODEL), layer),
            _layer_spec((D_MODEL, QKV_DIM), a),
            _layer_spec((1, QKV_DIM), a),
            _layer_spec((Q_DIM, D_MODEL), a),
            _const_spec((N_KV_HEADS, PAIR_BAND, GROUP * PAIR)),
            _layer_spec((N_KV_HEADS, 1, GROUP * PAIR), a),
        ],
        out_specs=[
            pl.BlockSpec((MIX_TILE, D_MODEL), lambda b, t: (b * tiles + t, 0)),
            kv_spec, kv_spec,
        ],
        out_shape=[jax.ShapeDtypeStruct((ROWS, D_MODEL), F32), kv_shape, kv_shape],
        scratch_shapes=[
            pltpu.VMEM((N_KV_HEADS, MIX_TILE // PAIR, GROUP * PAIR, LANES), BF16),
            pltpu.VMEM((KV_DIM // LANES, WINDOW + MIX_TILE, LANES), BF16),
            pltpu.VMEM((N_KV_HEADS, HEAD_DIM + ONES_ROWS, WINDOW + MIX_TILE), BF16),
            pltpu.VMEM((Q_DIM, MIX_TILE), BF16),
        ],
        input_output_aliases={0: 0},
        compiler_params=_params(("arbitrary", "arbitrary")),
        name="attn_prompt",
    )(x, g, w_qkv, b_qkv, w_o, bias, sinks)


def _stage_kv(k, v, kp_ref, v2_ref, row0):
    rows = k.shape[0]
    lane = lax.broadcasted_iota(jnp.int32, (rows, LANES), 1)
    for p in range(KV_DIM // LANES):
        cols = slice(p * LANES, (p + 1) * LANES)
        kp_ref[p, row0:row0 + rows, :] = k[:, cols].astype(BF16)
        pair = v[:, cols]
        lo_even = jnp.where(lane < HEAD_DIM, pair, 0.0)
        hi_odd = jnp.where(lane >= HEAD_DIM, pair, 0.0)
        hi_even = pltpu.roll(lo_even, HEAD_DIM, 1)
        lo_odd = pltpu.roll(hi_odd, HEAD_DIM, 1)
        for kh, lo, hi in ((2 * p, lo_even, hi_even), (2 * p + 1, lo_odd, hi_odd)):
            v2_ref[kh, row0:row0 + rows, 0:LANES] = lo.astype(BF16)
            v2_ref[kh, row0:row0 + rows, LANES:2 * LANES] = hi.astype(BF16)


def _attend_rows(qz, kband, vband, bias, sink):
    t = qz.shape[0] // GROUP
    s = _dot_nt(qz, kband) + bias
    m = jnp.maximum(jnp.max(s, axis=-1, keepdims=True), sink)
    p = jnp.exp(s - m).astype(BF16)
    o2 = _dot(p, vband)
    inv = 1.0 / (_dot(p, jnp.ones((p.shape[1], LANES), BF16)) + jnp.exp(sink - m))
    o2 = o2 * jnp.concatenate([inv, inv], axis=1)
    return (o2[0:t, 0:LANES] + o2[t:2 * t, LANES:2 * LANES],
            o2[2 * t:3 * t, 0:LANES] + o2[3 * t:4 * t, LANES:2 * LANES])


def _attn_sample_body(x_ref, ck_ref, cv_ref, g_ref, wqkv_ref, bqkv_ref, wo_ref, bias_ref, sink_ref,
                      o_ref, nk_ref, nv_ref, qz_ref, kp_ref, v2_ref, att_ref):
    x = x_ref[...]
    h = _rmsnorm(x, g_ref[...]).astype(BF16)
    qkv = _dot(h, wqkv_ref[...]) + bqkv_ref[...]
    q = qkv[:, :Q_DIM] * (HEAD_DIM ** -0.5)
    k = qkv[:, Q_DIM:Q_DIM + KV_DIM]
    v = qkv[:, Q_DIM + KV_DIM:]
    for b in range(DEC_BATCH):
        rows = slice(b * DEC_SEQ, (b + 1) * DEC_SEQ)
        ck = ck_ref[b]
        cv = cv_ref[b]
        nk_ref[b, 0:WINDOW - DEC_SEQ, :] = ck[DEC_SEQ:]
        nk_ref[b, WINDOW - DEC_SEQ:, :] = k[rows]
        nv_ref[b, 0:WINDOW - DEC_SEQ, :] = cv[DEC_SEQ:]
        nv_ref[b, WINDOW - DEC_SEQ:, :] = v[rows]
        _stage_q(q[rows], qz_ref.at[b], DEC_SEQ)
        _stage_kv(ck, cv, kp_ref.at[b], v2_ref.at[b], 0)
        _stage_kv(k[rows], v[rows], kp_ref.at[b], v2_ref.at[b], WINDOW)
        pairs = []
        for kh in range(N_KV_HEADS):
            pairs.extend(_attend_rows(qz_ref[b, kh, 0], kp_ref[b, kh // 2], v2_ref[b, kh], bias_ref[kh], sink_ref[kh]))
        att_ref[rows, :] = jnp.concatenate(pairs, axis=1).astype(BF16)
    o_ref[...] = x + _dot(att_ref[...], wo_ref[...])


def _attn_sample(x, ck, cv, g, w_qkv, b_qkv, w_o, bias, sinks, layer, a):
    nkeys = WINDOW + DEC_SEQ
    kv_shape = jax.ShapeDtypeStruct((DEC_BATCH, WINDOW, KV_DIM), F32)
    kv_spec = pl.BlockSpec((DEC_BATCH, WINDOW, KV_DIM), lambda i: (0, 0, 0))
    x_spec = pl.BlockSpec((SAMPLE_ROWS, D_MODEL), lambda i: (SAMPLE_BLOCK, 0))
    return pl.pallas_call(
        _attn_sample_body,
        grid=(1,),
        in_specs=[
            x_spec,
            _layer_spec((DEC_BATCH, WINDOW, KV_DIM), a),
            _layer_spec((DEC_BATCH, WINDOW, KV_DIM), a),
            _layer_spec((1, D_MODEL), layer),
            _layer_spec((D_MODEL, QKV_DIM), a),
            _layer_spec((1, QKV_DIM), a),
            _layer_spec((Q_DIM, D_MODEL), a),
            _const_spec((N_KV_HEADS, GROUP * DEC_SEQ, nkeys)),
            _layer_spec((N_KV_HEADS, GROUP * DEC_SEQ, 1), a),
        ],
        out_specs=[x_spec, kv_spec, kv_spec],
        out_shape=[jax.ShapeDtypeStruct((ROWS, D_MODEL), F32), kv_shape, kv_shape],
        scratch_shapes=[
            pltpu.VMEM((DEC_BATCH, N_KV_HEADS, 1, GROUP * DEC_SEQ, LANES), BF16),
            pltpu.VMEM((DEC_BATCH, KV_DIM // LANES, nkeys, LANES), BF16),
            pltpu.VMEM((DEC_BATCH, N_KV_HEADS, nkeys, 2 * LANES), BF16),
            pltpu.VMEM((SAMPLE_ROWS, Q_DIM), BF16),
        ],
        input_output_aliases={0: 0},
        compiler_params=_params(("arbitrary",)),
        name="attn_sample",
    )(x, ck, cv, g, w_qkv, b_qkv, w_o, bias, sinks)


def _glu_rows(x, g_ref, win_ref, bin_ref):
    h = _rmsnorm(x, g_ref[...]).astype(BF16)
    ag = _dot(h, win_ref[...]) + bin_ref[...]
    return ag[:, :D_MODEL] * jax.nn.sigmoid(ag[:, D_MODEL:])


def _conv_block(ubuf_ref, src0, dst0, rb, wdw_ref, bdw_ref, lng_ref, lnb_ref, conv_ref, y_ref):
    for c in range(D_MODEL // LANES):
        cols = slice(c * LANES, (c + 1) * LANES)
        acc = jnp.broadcast_to(bdw_ref[:, cols], (rb, LANES))
        for w in range(CONV_WIDTH):
            tap0 = src0 + HIST - CONV_PAD + w
            acc = acc + ubuf_ref[c, tap0:tap0 + rb, :] * wdw_ref[w:w + 1, cols]
        conv_ref[dst0:dst0 + rb, cols] = acc
    yc = conv_ref[dst0:dst0 + rb, :]
    xc = yc - jnp.mean(yc, axis=-1, keepdims=True)
    yn = xc * lax.rsqrt(jnp.mean(xc * xc, axis=-1, keepdims=True) + NORM_EPS)
    yn = yn * lng_ref[...] + lnb_ref[...]
    y_ref[dst0:dst0 + rb, :] = (yn * jax.nn.sigmoid(yn)).astype(BF16)


def _conv_prompt_body(x_ref, g_ref, win_ref, bin_ref, wdw_ref, bdw_ref, lng_ref, lnb_ref, wout_ref,
                      bout_ref, o_ref, st_ref, ubuf_ref, conv_ref, y_ref):
    @pl.when(pl.program_id(1) == 0)
    def _():
        ubuf_ref[:, 0:HIST, :] = jnp.zeros((D_MODEL // LANES, HIST, LANES), F32)

    x = x_ref[...]
    u = _glu_rows(x, g_ref, win_ref, bin_ref)
    st_ref[...] = u[MIX_TILE - HIST:]
    for c in range(D_MODEL // LANES):
        ubuf_ref[c, HIST:, :] = u[:, c * LANES:(c + 1) * LANES]
    for r0 in range(0, MIX_TILE, CONV_ROWS):
        _conv_block(ubuf_ref, r0, r0, CONV_ROWS, wdw_ref, bdw_ref, lng_ref, lnb_ref, conv_ref, y_ref)
    o_ref[...] = x + _dot(y_ref[...], wout_ref[...]) + bout_ref[...]
    ubuf_ref[:, 0:HIST, :] = ubuf_ref[:, MIX_TILE:MIX_TILE + HIST, :]


def _conv_sample_body(x_ref, hist_ref, g_ref, win_ref, bin_ref, wdw_ref, bdw_ref, lng_ref, lnb_ref,
                      wout_ref, bout_ref, o_ref, st_ref, ubuf_ref, conv_ref, y_ref):
    x = x_ref[...]
    u = _glu_rows(x, g_ref, win_ref, bin_ref)
    for b in range(DEC_BATCH):
        rows = slice(b * DEC_SEQ, (b + 1) * DEC_SEQ)
        st_ref[b] = u[rows]
        hist = hist_ref[b]
        for c in range(D_MODEL // LANES):
            cols = slice(c * LANES, (c + 1) * LANES)
            ubuf_ref[b, c, 0:HIST, :] = hist[:, cols]
            ubuf_ref[b, c, HIST:, :] = u[rows, cols]
        _conv_block(ubuf_ref.at[b], 0, b * DEC_SEQ, DEC_SEQ, wdw_ref, bdw_ref, lng_ref, lnb_ref, conv_ref, y_ref)
    o_ref[...] = x + _dot(y_ref[...], wout_ref[...]) + bout_ref[...]


def _conv_weight_specs(layer, c):
    return [
        _layer_spec((1, D_MODEL), layer),
        _layer_spec((D_MODEL, 2 * D_MODEL), c),
        _layer_spec((1, 2 * D_MODEL), c),
        _layer_spec((CONV_WIDTH, D_MODEL), c),
        _layer_spec((1, D_MODEL), c),
        _layer_spec((1, D_MODEL), c),
        _layer_spec((1, D_MODEL), c),
        _layer_spec((D_MODEL, D_MODEL), c),
        _layer_spec((1, D_MODEL), c),
    ]


def _conv_prompt(x, weights, layer, c):
    tiles = SEQ // MIX_TILE
    return pl.pallas_call(
        _conv_prompt_body,
        grid=(BATCH, tiles),
        in_specs=[pl.BlockSpec((MIX_TILE, D_MODEL), lambda b, t: (b * tiles + t, 0))] + _conv_weight_specs(layer, c),
        out_specs=[
            pl.BlockSpec((MIX_TILE, D_MODEL), lambda b, t: (b * tiles + t, 0)),
            pl.BlockSpec((None, HIST, D_MODEL), lambda b, t: (b, 0, 0)),
        ],
        out_shape=[jax.ShapeDtypeStruct((ROWS, D_MODEL), F32),
                   jax.ShapeDtypeStruct((BATCH, HIST, D_MODEL), F32)],
        scratch_shapes=[
            pltpu.VMEM((D_MODEL // LANES, HIST + MIX_TILE, LANES), F32),
            pltpu.VMEM((MIX_TILE, D_MODEL), F32),
            pltpu.VMEM((MIX_TILE, D_MODEL), BF16),
        ],
        input_output_aliases={0: 0},
        compiler_params=_params(("arbitrary", "arbitrary")),
        name="conv_prompt",
    )(x, *weights)


def _conv_sample(x, hist, weights, layer, c):
    x_spec = pl.BlockSpec((SAMPLE_ROWS, D_MODEL), lambda i: (SAMPLE_BLOCK, 0))
    return pl.pallas_call(
        _conv_sample_body,
        grid=(1,),
        in_specs=[x_spec, _layer_spec((DEC_BATCH, HIST, D_MODEL), c)] + _conv_weight_specs(layer, c),
        out_specs=[x_spec, pl.BlockSpec((DEC_BATCH, HIST, D_MODEL), lambda i: (0, 0, 0))],
        out_shape=[jax.ShapeDtypeStruct((ROWS, D_MODEL), F32),
                   jax.ShapeDtypeStruct((DEC_BATCH, HIST, D_MODEL), F32)],
        scratch_shapes=[
            pltpu.VMEM((DEC_BATCH, D_MODEL // LANES, HIST + DEC_SEQ, LANES), F32),
            pltpu.VMEM((SAMPLE_ROWS, D_MODEL), F32),
            pltpu.VMEM((SAMPLE_ROWS, D_MODEL), BF16),
        ],
        input_output_aliases={0: 0},
        compiler_params=_params(("arbitrary",)),
        name="conv_sample",
    )(x, hist, *weights)


def _alibi_slopes():
    return np.array([2.0 ** (-8.0 * (h + 1) / N_HEADS) for h in range(N_HEADS)], dtype=np.float32)


def _sample_bias():
    i = np.arange(DEC_SEQ)[:, None]
    j = np.arange(WINDOW + DEC_SEQ)[None, :]
    dist = np.abs(i - (j - WINDOW)).astype(np.float32)
    bias = -_alibi_slopes()[:, None, None] * dist
    return jnp.asarray(bias.reshape(N_KV_HEADS, GROUP * DEC_SEQ, WINDOW + DEC_SEQ))


def _pair_bias():
    r = np.arange(PAIR_BAND)[:, None, None]
    cc = np.arange(PAIR // CHUNK)[None, :, None]
    i = np.arange(CHUNK)[None, None, :]
    j = r - cc * CHUNK
    dist = np.abs(i - (j - WINDOW)).astype(np.float32)
    bias = np.float32(LOG2E) * -_alibi_slopes()[:, None, None, None] * dist[None]
    bias = np.where((j >= 0) & (j < BAND), bias, np.float32(NEG_INF)).astype(np.float32)
    bias = bias.reshape(N_KV_HEADS, GROUP, PAIR_BAND, PAIR).transpose(0, 2, 1, 3)
    return jnp.asarray(bias.reshape(N_KV_HEADS, PAIR_BAND, GROUP * PAIR))


def _sink_lanes(sinks):
    s = LOG2E * sinks.astype(F32).reshape(-1, N_KV_HEADS, GROUP)
    return jnp.repeat(s, PAIR, axis=2)[:, :, None, :]


def _sink_rows(sinks):
    s = sinks.astype(F32).reshape(-1, N_KV_HEADS, GROUP)
    return jnp.repeat(s, DEC_SEQ, axis=2)[:, :, :, None]


def kernel(x_prompt, x_sample, cache_k, cache_v, state_conv, norm_ffn1, norm_mix, norm_ffn2, norm_final,
           ffn1_w_in, ffn1_w_out, ffn2_w_in, ffn2_w_out, attn_w_qkv, attn_b_qkv, attn_w_o, attn_sinks,
           conv_w_in, conv_b_in, conv_w_dw, conv_b_dw, conv_ln_g, conv_ln_b, conv_w_out, conv_b_out):
    bf = lambda a: a.astype(BF16)
    rows = lambda a: a.astype(F32).reshape(a.shape[0], 1, a.shape[-1])
    xs = (x_prompt.reshape(PROMPT_ROWS, D_MODEL), x_sample.reshape(SAMPLE_ROWS, D_MODEL))
    g_ffn1, g_mix, g_ffn2 = rows(norm_ffn1), rows(norm_mix), rows(norm_ffn2)
    g_final = norm_final.astype(F32).reshape(1, D_MODEL)
    w1_in, w1_out, w2_in, w2_out = bf(ffn1_w_in), bf(ffn1_w_out), bf(ffn2_w_in), bf(ffn2_w_out)
    attn_w = (bf(attn_w_qkv), rows(attn_b_qkv), bf(attn_w_o))
    bias_p, sinks_p = _pair_bias(), _sink_lanes(attn_sinks)
    bias_s, sinks_s = _sample_bias(), _sink_rows(attn_sinks)
    ck = cache_k.reshape(-1, DEC_BATCH, WINDOW, KV_DIM)
    cv = cache_v.reshape(-1, DEC_BATCH, WINDOW, KV_DIM)
    conv_w = (bf(conv_w_in), rows(conv_b_in), conv_w_dw.astype(F32), rows(conv_b_dw), rows(conv_ln_g),
              rows(conv_ln_b), bf(conv_w_out), rows(conv_b_out))
    hist = jnp.pad(state_conv, ((0, 0), (0, 0), (HIST - CONV_PAD, 0), (0, 0)))

    new_k_p, new_v_p, new_k_s, new_v_s, new_c_p, new_c_s = [], [], [], [], [], []
    for layer in range(DEPTH):
        x = _ffn(xs if layer == 0 else (x,), g_ffn1, w1_in, w1_out, layer, g_final, first=layer == 0)
        if layer % 2 == 0:
            a = layer // 2
            x, nk, nv = _attn_prompt(x, g_mix, *attn_w, bias_p, sinks_p, layer, a)
            new_k_p.append(nk)
            new_v_p.append(nv)
            x, nk, nv = _attn_sample(x, ck, cv, g_mix, *attn_w, bias_s, sinks_s, layer, a)
            new_k_s.append(nk)
            new_v_s.append(nv)
        else:
            c = layer // 2
            x, st = _conv_prompt(x, (g_mix,) + conv_w, layer, c)
            new_c_p.append(st[:, HIST - CONV_PAD:])
            x, st = _conv_sample(x, hist, (g_mix,) + conv_w, layer, c)
            new_c_s.append(st[:, HIST - CONV_PAD:])
        x = _ffn((x,), g_ffn2, w2_in, w2_out, layer, g_final, last=layer == DEPTH - 1)

    y_prompt, y_sample = x
    kv5 = lambda parts, b: jnp.stack(parts).reshape(len(parts), b, WINDOW, N_KV_HEADS, HEAD_DIM)
    return (y_prompt.reshape(BATCH, SEQ, D_MODEL),
            y_sample.reshape(DEC_BATCH, DEC_SEQ, D_MODEL),
            kv5(new_k_p, BATCH), kv5(new_v_p, BATCH), jnp.stack(new_c_p),
            kv5(new_k_s, DEC_BATCH), kv5(new_v_s, DEC_BATCH), jnp.stack(new_c_s))
```

```python
import functools

import numpy as np
import jax
import jax.numpy as jnp
from jax import lax
from jax.experimental import pallas as pl
from jax.experimental.pallas import tpu as pltpu

D_MODEL = 1024
BATCH = 16
SEQ = 2048
DEPTH = 4
DEC_BATCH = 16
DEC_SEQ = 32
CHUNK = 64
N_HEADS = 16
N_KV_HEADS = 4
HEAD_DIM = 64
GROUP = N_HEADS // N_KV_HEADS
Q_DIM = N_HEADS * HEAD_DIM
KV_DIM = N_KV_HEADS * HEAD_DIM
QKV_DIM = Q_DIM + 2 * KV_DIM
WINDOW = 128
BAND = WINDOW + CHUNK
CONV_WIDTH = 31
CONV_PAD = CONV_WIDTH - 1
FFN_DIM = 2816
NORM_EPS = 1e-5
NEG_INF = -1e30
LOG2E = 1.4426950408889634

F32 = jnp.float32
BF16 = jnp.bfloat16

LANES = 128
PROMPT_ROWS = BATCH * SEQ
SAMPLE_ROWS = DEC_BATCH * DEC_SEQ
ROWS = PROMPT_ROWS + SAMPLE_ROWS
FFN_TILE = 512
FFN_CHUNK = 256
MIX_TILE = 1024
PROMPT_TILES = PROMPT_ROWS // FFN_TILE
SAMPLE_BLOCK = PROMPT_ROWS // SAMPLE_ROWS
PAIR = 2 * CHUNK
PAIR_BAND = WINDOW + PAIR
ONES_ROWS = 16
ATTN_LAG = 1
HIST = 32
CONV_ROWS = 64
VMEM_LIMIT = 56 * 1024 * 1024

assert ROWS % FFN_TILE == 0 and SEQ % MIX_TILE == 0 and MIX_TILE % PAIR == 0
assert PROMPT_ROWS % SAMPLE_ROWS == 0 and SAMPLE_ROWS % FFN_TILE == 0
assert FFN_DIM % FFN_CHUNK == 0 and HIST >= CONV_PAD and DEC_SEQ == HIST


def _rmsnorm(x, g):
    return x * lax.rsqrt(jnp.mean(x * x, axis=-1, keepdims=True) + NORM_EPS) * g


def _dot(a, b):
    return jnp.dot(a, b, preferred_element_type=F32)


def _dot_nt(a, b):
    return lax.dot_general(a, b, (((1,), (1,)), ((), ())), preferred_element_type=F32)


def _const_spec(shape):
    return pl.BlockSpec(shape, lambda *_: (0,) * len(shape), pipeline_mode=pl.Buffered(1))


def _layer_spec(shape, layer):
    return pl.BlockSpec((None,) + shape, lambda *_: (layer,) + (0,) * len(shape), pipeline_mode=pl.Buffered(1))


def _params(dimension_semantics):
    return pltpu.CompilerParams(dimension_semantics=dimension_semantics, vmem_limit_bytes=VMEM_LIMIT)


def _ffn_body(*refs, first, last):
    refs = list(refs)
    i = pl.program_id(0)
    act_ref = refs.pop()
    if first:
        xp_ref, xs_ref = refs.pop(0), refs.pop(0)
        x = jnp.where(i < PROMPT_TILES, xp_ref[...], xs_ref[...])
    else:
        x = refs.pop(0)[...]
    g_ref, win_ref, wout_ref, gfin_ref, *out_refs = refs
    h = _rmsnorm(x, g_ref[...]).astype(BF16)
    for c in range(FFN_DIM // FFN_CHUNK):
        lo = c * FFN_CHUNK
        gate = _dot(h, win_ref[:, lo:lo + FFN_CHUNK].astype(BF16))
        up = _dot(h, win_ref[:, FFN_DIM + lo:FFN_DIM + lo + FFN_CHUNK].astype(BF16))
        act_ref[:, lo:lo + FFN_CHUNK] = (gate * jax.nn.sigmoid(gate) * up).astype(BF16)
    y = x + 0.5 * _dot(act_ref[...], wout_ref[...].astype(BF16))
    if last:
        y = _rmsnorm(y, gfin_ref[...])
        yp_ref, ys_ref = out_refs

        @pl.when(i < PROMPT_TILES)
        def _():
            yp_ref[...] = y

        @pl.when(i >= PROMPT_TILES)
        def _():
            ys_ref[...] = y
    else:
        out_refs[0][...] = y


def _ffn(xs, g, w_in, w_out, layer, g_final, first=False, last=False):
    tile = (FFN_TILE, D_MODEL)
    prompt_spec = pl.BlockSpec(tile, lambda i: (jnp.minimum(i, PROMPT_TILES - 1), 0))
    sample_spec = pl.BlockSpec(tile, lambda i: (jnp.maximum(i - PROMPT_TILES, 0), 0))
    joint_spec = pl.BlockSpec(tile, lambda i: (i, 0))
    split_shape = [jax.ShapeDtypeStruct((PROMPT_ROWS, D_MODEL), F32),
                   jax.ShapeDtypeStruct((SAMPLE_ROWS, D_MODEL), F32)]
    return pl.pallas_call(
        functools.partial(_ffn_body, first=first, last=last),
        grid=(ROWS // FFN_TILE,),
        in_specs=([prompt_spec, sample_spec] if first else [joint_spec]) + [
            _layer_spec((1, D_MODEL), layer),
            _layer_spec((D_MODEL, 2 * FFN_DIM), layer),
            _layer_spec((FFN_DIM, D_MODEL), layer),
            _const_spec((1, D_MODEL)),
        ],
        out_specs=[prompt_spec, sample_spec] if last else joint_spec,
        out_shape=split_shape if last else jax.ShapeDtypeStruct((ROWS, D_MODEL), F32),
        scratch_shapes=[pltpu.VMEM((FFN_TILE, FFN_DIM), BF16)],
        compiler_params=_params(("arbitrary",)),
        name="ffn_first" if first else "ffn_last" if last else "ffn",
    )(*xs, g, w_in, w_out, g_final)


def _stage_q(q, qz_ref, chunk):
    rows = q.shape[0]
    lane = lax.broadcasted_iota(jnp.int32, (rows, LANES), 1)
    for h in range(N_HEADS):
        kh, g = divmod(h, GROUP)
        blk = q[:, (h // 2) * LANES:(h // 2 + 1) * LANES]
        if h % 2 != kh % 2:
            blk = pltpu.roll(blk, HEAD_DIM, 1)
        keep = (lane < HEAD_DIM) if kh % 2 == 0 else (lane >= HEAD_DIM)
        blk = jnp.where(keep, blk, 0.0).astype(BF16)
        for c in range(rows // chunk):
            qz_ref[kh, c, g * chunk:(g + 1) * chunk, :] = blk[c * chunk:(c + 1) * chunk]


def _attn_prompt_body(x_ref, g_ref, wqkv_ref, bqkv_ref, wo_ref, bias_ref, sink_ref,
                      o_ref, nk_ref, nv_ref, qz_ref, kp_ref, vt_ref, att_ref):
    t = pl.program_id(1)

    @pl.when(t == 0)
    def _():
        kp_ref[:, 0:WINDOW, :] = jnp.zeros((KV_DIM // LANES, WINDOW, LANES), BF16)
        vt_ref[:, 0:HEAD_DIM, 0:WINDOW] = jnp.zeros((N_KV_HEADS, HEAD_DIM, WINDOW), BF16)
        vt_ref[:, HEAD_DIM:, 0:WINDOW] = jnp.ones((N_KV_HEADS, ONES_ROWS, WINDOW), BF16)

    x = x_ref[...]
    h = _rmsnorm(x, g_ref[...]).astype(BF16)
    qkv = _dot(h, wqkv_ref[...]) + bqkv_ref[...]
    k = qkv[:, Q_DIM:Q_DIM + KV_DIM]
    v = qkv[:, Q_DIM + KV_DIM:]
    nk_ref[...] = k[MIX_TILE - WINDOW:]
    nv_ref[...] = v[MIX_TILE - WINDOW:]
    _stage_q(qkv[:, :Q_DIM] * (LOG2E * HEAD_DIM ** -0.5), qz_ref, PAIR)
    for p in range(KV_DIM // LANES):
        kp_ref[p, WINDOW:, :] = k[:, p * LANES:(p + 1) * LANES].astype(BF16)
    v_t = v.T
    for kh in range(N_KV_HEADS):
        vt_ref[kh, 0:HEAD_DIM, WINDOW:] = v_t[kh * HEAD_DIM:(kh + 1) * HEAD_DIM, :].astype(BF16)
        vt_ref[kh, HEAD_DIM:, WINDOW:] = jnp.ones((ONES_ROWS, MIX_TILE), BF16)

    key_row = lax.broadcasted_iota(jnp.int32, (PAIR_BAND, GROUP * PAIR), 0)
    units = [(pp, kh) for pp in range(MIX_TILE // PAIR) for kh in range(N_KV_HEADS)]
    keys = lambda pp: slice(pp * PAIR, pp * PAIR + PAIR_BAND)

    def scores(pp, kh):
        s = _dot_nt(kp_ref[kh // 2, keys(pp), :], qz_ref[kh, pp]) + bias_ref[kh]
        if pp == 0:
            s = jnp.where(key_row >= WINDOW - t * MIX_TILE, s, NEG_INF)
        return s, jnp.maximum(jnp.max(s, axis=0, keepdims=True), sink_ref[kh])

    def finish(pp, kh, p, sink_p):
        o = _dot(vt_ref[kh, :, keys(pp)], p)
        o = o[0:HEAD_DIM, :] * (1.0 / (o[HEAD_DIM:HEAD_DIM + 1, :] + sink_p))
        for g in range(GROUP):
            head = kh * GROUP + g
            att_ref[head * HEAD_DIM:(head + 1) * HEAD_DIM, pp * PAIR:(pp + 1) * PAIR] = (
                o[:, g * PAIR:(g + 1) * PAIR].astype(BF16))

    scored, exped = {}, {}
    for step in range(len(units) + 2 * ATTN_LAG):
        if step < len(units):
            scored[step] = scores(*units[step])
        u = step - ATTN_LAG
        if 0 <= u < len(units):
            s, m = scored.pop(u)
            exped[u] = jnp.exp2(s - m).astype(BF16), jnp.exp2(sink_ref[units[u][1]] - m)
        u = step - 2 * ATTN_LAG
        if u >= 0:
            finish(*units[u], *exped.pop(u))

    out = lax.dot_general(att_ref[...], wo_ref[...], (((0,), (0,)), ((), ())), preferred_element_type=F32)
    o_ref[...] = x + out
    kp_ref[:, 0:WINDOW, :] = kp_ref[:, MIX_TILE:MIX_TILE + WINDOW, :]
    vt_ref[:, :, 0:WINDOW] = vt_ref[:, :, MIX_TILE:MIX_TILE + WINDOW]


def _attn_prompt(x, g, w_qkv, b_qkv, w_o, bias, sinks, layer, a):
    tiles = SEQ // MIX_TILE
    kv_shape = jax.ShapeDtypeStruct((BATCH, WINDOW, KV_DIM), F32)
    kv_spec = pl.BlockSpec((None, WINDOW, KV_DIM), lambda b, t: (b, 0, 0))
    return pl.pallas_call(
        _attn_prompt_body,
        grid=(BATCH, tiles),
        in_specs=[
            pl.BlockSpec((MIX_TILE, D_MODEL), lambda b, t: (b * tiles + t, 0)),
            _layer_spec((1, D_MODEL), layer),
            _layer_spec((D_MODEL, QKV_DIM), a),
            _layer_spec((1, QKV_DIM), a),
            _layer_spec((Q_DIM, D_MODEL), a),
            _const_spec((N_KV_HEADS, PAIR_BAND, GROUP * PAIR)),
            _layer_spec((N_KV_HEADS, 1, GROUP * PAIR), a),
        ],
        out_specs=[
            pl.BlockSpec((MIX_TILE, D_MODEL), lambda b, t: (b * tiles + t, 0)),
            kv_spec, kv_spec,
        ],
        out_shape=[jax.ShapeDtypeStruct((ROWS, D_MODEL), F32), kv_shape, kv_shape],
        scratch_shapes=[
            pltpu.VMEM((N_KV_HEADS, MIX_TILE // PAIR, GROUP * PAIR, LANES), BF16),
            pltpu.VMEM((KV_DIM // LANES, WINDOW + MIX_TILE, LANES), BF16),
            pltpu.VMEM((N_KV_HEADS, HEAD_DIM + ONES_ROWS, WINDOW + MIX_TILE), BF16),
            pltpu.VMEM((Q_DIM, MIX_TILE), BF16),
        ],
        input_output_aliases={0: 0},
        compiler_params=_params(("arbitrary", "arbitrary")),
        name="attn_prompt",
    )(x, g, w_qkv, b_qkv, w_o, bias, sinks)


def _stage_kv(k, v, kp_ref, v2_ref, row0):
    rows = k.shape[0]
    lane = lax.broadcasted_iota(jnp.int32, (rows, LANES), 1)
    for p in range(KV_DIM // LANES):
        cols = slice(p * LANES, (p + 1) * LANES)
        kp_ref[p, row0:row0 + rows, :] = k[:, cols].astype(BF16)
        pair = v[:, cols]
        lo_even = jnp.where(lane < HEAD_DIM, pair, 0.0)
        hi_odd = jnp.where(lane >= HEAD_DIM, pair, 0.0)
        hi_even = pltpu.roll(lo_even, HEAD_DIM, 1)
        lo_odd = pltpu.roll(hi_odd, HEAD_DIM, 1)
        for kh, lo, hi in ((2 * p, lo_even, hi_even), (2 * p + 1, lo_odd, hi_odd)):
            v2_ref[kh, row0:row0 + rows, 0:LANES] = lo.astype(BF16)
            v2_ref[kh, row0:row0 + rows, LANES:2 * LANES] = hi.astype(BF16)


def _attend_rows(qz, kband, vband, bias, sink):
    t = qz.shape[0] // GROUP
    s = _dot_nt(qz, kband) + bias
    m = jnp.maximum(jnp.max(s, axis=-1, keepdims=True), sink)
    p = jnp.exp(s - m).astype(BF16)
    o2 = _dot(p, vband)
    inv = 1.0 / (_dot(p, jnp.ones((p.shape[1], LANES), BF16)) + jnp.exp(sink - m))
    o2 = o2 * jnp.concatenate([inv, inv], axis=1)
    return (o2[0:t, 0:LANES] + o2[t:2 * t, LANES:2 * LANES],
            o2[2 * t:3 * t, 0:LANES] + o2[3 * t:4 * t, LANES:2 * LANES])


def _attn_sample_body(x_ref, ck_ref, cv_ref, g_ref, wqkv_ref, bqkv_ref, wo_ref, bias_ref, sink_ref,
                      o_ref, nk_ref, nv_ref, qz_ref, kp_ref, v2_ref, att_ref):
    x = x_ref[...]
    h = _rmsnorm(x, g_ref[...]).astype(BF16)
    qkv = _dot(h, wqkv_ref[...]) + bqkv_ref[...]
    q = qkv[:, :Q_DIM] * (HEAD_DIM ** -0.5)
    k = qkv[:, Q_DIM:Q_DIM + KV_DIM]
    v = qkv[:, Q_DIM + KV_DIM:]
    for b in range(DEC_BATCH):
        rows = slice(b * DEC_SEQ, (b + 1) * DEC_SEQ)
        ck = ck_ref[b]
        cv = cv_ref[b]
        nk_ref[b, 0:WINDOW - DEC_SEQ, :] = ck[DEC_SEQ:]
        nk_ref[b, WINDOW - DEC_SEQ:, :] = k[rows]
        nv_ref[b, 0:WINDOW - DEC_SEQ, :] = cv[DEC_SEQ:]
        nv_ref[b, WINDOW - DEC_SEQ:, :] = v[rows]
        _stage_q(q[rows], qz_ref.at[b], DEC_SEQ)
        _stage_kv(ck, cv, kp_ref.at[b], v2_ref.at[b], 0)
        _stage_kv(k[rows], v[rows], kp_ref.at[b], v2_ref.at[b], WINDOW)
        pairs = []
        for kh in range(N_KV_HEADS):
            pairs.extend(_attend_rows(qz_ref[b, kh, 0], kp_ref[b, kh // 2], v2_ref[b, kh], bias_ref[kh], sink_ref[kh]))
        att_ref[rows, :] = jnp.concatenate(pairs, axis=1).astype(BF16)
    o_ref[...] = x + _dot(att_ref[...], wo_ref[...])


def _attn_sample(x, ck, cv, g, w_qkv, b_qkv, w_o, bias, sinks, layer, a):
    nkeys = WINDOW + DEC_SEQ
    kv_shape = jax.ShapeDtypeStruct((DEC_BATCH, WINDOW, KV_DIM), F32)
    kv_spec = pl.BlockSpec((DEC_BATCH, WINDOW, KV_DIM), lambda i: (0, 0, 0))
    x_spec = pl.BlockSpec((SAMPLE_ROWS, D_MODEL), lambda i: (SAMPLE_BLOCK, 0))
    return pl.pallas_call(
        _attn_sample_body,
        grid=(1,),
        in_specs=[
            x_spec,
            _layer_spec((DEC_BATCH, WINDOW, KV_DIM), a),
            _layer_spec((DEC_BATCH, WINDOW, KV_DIM), a),
            _layer_spec((1, D_MODEL), layer),
            _layer_spec((D_MODEL, QKV_DIM), a),
            _layer_spec((1, QKV_DIM), a),
            _layer_spec((Q_DIM, D_MODEL), a),
            _const_spec((N_KV_HEADS, GROUP * DEC_SEQ, nkeys)),
            _layer_spec((N_KV_HEADS, GROUP * DEC_SEQ, 1), a),
        ],
        out_specs=[x_spec, kv_spec, kv_spec],
        out_shape=[jax.ShapeDtypeStruct((ROWS, D_MODEL), F32), kv_shape, kv_shape],
        scratch_shapes=[
            pltpu.VMEM((DEC_BATCH, N_KV_HEADS, 1, GROUP * DEC_SEQ, LANES), BF16),
            pltpu.VMEM((DEC_BATCH, KV_DIM // LANES, nkeys, LANES), BF16),
            pltpu.VMEM((DEC_BATCH, N_KV_HEADS, nkeys, 2 * LANES), BF16),
            pltpu.VMEM((SAMPLE_ROWS, Q_DIM), BF16),
        ],
        input_output_aliases={0: 0},
        compiler_params=_params(("arbitrary",)),
        name="attn_sample",
    )(x, ck, cv, g, w_qkv, b_qkv, w_o, bias, sinks)


def _glu_cols(h, c, win_ref, bin_ref):
    cols = slice(c * 2 * LANES, (c + 1) * 2 * LANES)
    ag = _dot(h, win_ref[:, cols]) + bin_ref[:, cols]
    return ag[:, :LANES] * jax.nn.sigmoid(ag[:, LANES:])


def _conv_cols(ubuf_ref, c, src0, dst0, rb, wdw_ref, bdw_ref, conv_ref):
    cols = slice(c * LANES, (c + 1) * LANES)
    acc = jnp.broadcast_to(bdw_ref[:, cols], (rb, LANES))
    for w in range(CONV_WIDTH):
        tap0 = src0 + HIST - CONV_PAD + w
        acc = acc + ubuf_ref[c, tap0:tap0 + rb, :] * wdw_ref[w:w + 1, cols]
    conv_ref[dst0:dst0 + rb, cols] = acc


def _ln_silu(r0, rb, lng_ref, lnb_ref, conv_ref, y_ref):
    yc = conv_ref[r0:r0 + rb, :]
    xc = yc - jnp.mean(yc, axis=-1, keepdims=True)
    yn = xc * lax.rsqrt(jnp.mean(xc * xc, axis=-1, keepdims=True) + NORM_EPS)
    yn = yn * lng_ref[...] + lnb_ref[...]
    y_ref[r0:r0 + rb, :] = (yn * jax.nn.sigmoid(yn)).astype(BF16)


def _conv_prompt_body(x_ref, g_ref, win_ref, bin_ref, wdw_ref, bdw_ref, lng_ref, lnb_ref, wout_ref,
                      bout_ref, o_ref, st_ref, ubuf_ref, conv_ref, y_ref):
    @pl.when(pl.program_id(1) == 0)
    def _():
        ubuf_ref[:, 0:HIST, :] = jnp.zeros((D_MODEL // LANES, HIST, LANES), F32)

    x = x_ref[...]
    h = _rmsnorm(x, g_ref[...]).astype(BF16)
    for c in range(D_MODEL // LANES):
        u = _glu_cols(h, c, win_ref, bin_ref)
        st_ref[:, c * LANES:(c + 1) * LANES] = u[MIX_TILE - HIST:]
        ubuf_ref[c, HIST:, :] = u
        for r0 in range(0, MIX_TILE, CONV_ROWS):
            _conv_cols(ubuf_ref, c, r0, r0, CONV_ROWS, wdw_ref, bdw_ref, conv_ref)
    half = MIX_TILE // 2
    for r in (0, half):
        for r0 in range(r, r + half, CONV_ROWS):
            _ln_silu(r0, CONV_ROWS, lng_ref, lnb_ref, conv_ref, y_ref)
        o_ref[r:r + half, :] = x[r:r + half] + _dot(y_ref[r:r + half, :], wout_ref[...]) + bout_ref[...]
    ubuf_ref[:, 0:HIST, :] = ubuf_ref[:, MIX_TILE:MIX_TILE + HIST, :]


def _conv_sample_body(x_ref, hist_ref, g_ref, win_ref, bin_ref, wdw_ref, bdw_ref, lng_ref, lnb_ref,
                      wout_ref, bout_ref, o_ref, st_ref, ubuf_ref, conv_ref, y_ref):
    x = x_ref[...]
    h = _rmsnorm(x, g_ref[...]).astype(BF16)
    for c in range(D_MODEL // LANES):
        cols = slice(c * LANES, (c + 1) * LANES)
        u = _glu_cols(h, c, win_ref, bin_ref)
        for b in range(DEC_BATCH):
            rows = slice(b * DEC_SEQ, (b + 1) * DEC_SEQ)
            st_ref[b, :, cols] = u[rows]
            ubuf_ref[b, c, 0:HIST, :] = hist_ref[b, :, cols]
            ubuf_ref[b, c, HIST:, :] = u[rows]
            _conv_cols(ubuf_ref.at[b], c, 0, b * DEC_SEQ, DEC_SEQ, wdw_ref, bdw_ref, conv_ref)
    _ln_silu(0, SAMPLE_ROWS, lng_ref, lnb_ref, conv_ref, y_ref)
    o_ref[...] = x + _dot(y_ref[...], wout_ref[...]) + bout_ref[...]


def _conv_weight_specs(layer, c):
    return [
        _layer_spec((1, D_MODEL), layer),
        _layer_spec((D_MODEL, 2 * D_MODEL), c),
        _layer_spec((1, 2 * D_MODEL), c),
        _layer_spec((CONV_WIDTH, D_MODEL), c),
        _layer_spec((1, D_MODEL), c),
        _layer_spec((1, D_MODEL), c),
        _layer_spec((1, D_MODEL), c),
        _layer_spec((D_MODEL, D_MODEL), c),
        _layer_spec((1, D_MODEL), c),
    ]


def _conv_prompt(x, weights, layer, c):
    tiles = SEQ // MIX_TILE
    return pl.pallas_call(
        _conv_prompt_body,
        grid=(BATCH, tiles),
        in_specs=[pl.BlockSpec((MIX_TILE, D_MODEL), lambda b, t: (b * tiles + t, 0))] + _conv_weight_specs(layer, c),
        out_specs=[
            pl.BlockSpec((MIX_TILE, D_MODEL), lambda b, t: (b * tiles + t, 0)),
            pl.BlockSpec((None, HIST, D_MODEL), lambda b, t: (b, 0, 0)),
        ],
        out_shape=[jax.ShapeDtypeStruct((ROWS, D_MODEL), F32),
                   jax.ShapeDtypeStruct((BATCH, HIST, D_MODEL), F32)],
        scratch_shapes=[
            pltpu.VMEM((D_MODEL // LANES, HIST + MIX_TILE, LANES), F32),
            pltpu.VMEM((MIX_TILE, D_MODEL), F32),
            pltpu.VMEM((MIX_TILE, D_MODEL), BF16),
        ],
        input_output_aliases={0: 0},
        compiler_params=_params(("arbitrary", "arbitrary")),
        name="conv_prompt",
    )(x, *weights)


def _conv_sample(x, hist, weights, layer, c):
    x_spec = pl.BlockSpec((SAMPLE_ROWS, D_MODEL), lambda i: (SAMPLE_BLOCK, 0))
    return pl.pallas_call(
        _conv_sample_body,
        grid=(1,),
        in_specs=[x_spec, _layer_spec((DEC_BATCH, HIST, D_MODEL), c)] + _conv_weight_specs(layer, c),
        out_specs=[x_spec, pl.BlockSpec((DEC_BATCH, HIST, D_MODEL), lambda i: (0, 0, 0))],
        out_shape=[jax.ShapeDtypeStruct((ROWS, D_MODEL), F32),
                   jax.ShapeDtypeStruct((DEC_BATCH, HIST, D_MODEL), F32)],
        scratch_shapes=[
            pltpu.VMEM((DEC_BATCH, D_MODEL // LANES, HIST + DEC_SEQ, LANES), F32),
            pltpu.VMEM((SAMPLE_ROWS, D_MODEL), F32),
            pltpu.VMEM((SAMPLE_ROWS, D_MODEL), BF16),
        ],
        input_output_aliases={0: 0},
        compiler_params=_params(("arbitrary",)),
        name="conv_sample",
    )(x, hist, *weights)


def _alibi_slopes():
    return np.array([2.0 ** (-8.0 * (h + 1) / N_HEADS) for h in range(N_HEADS)], dtype=np.float32)


def _sample_bias():
    i = np.arange(DEC_SEQ)[:, None]
    j = np.arange(WINDOW + DEC_SEQ)[None, :]
    dist = np.abs(i - (j - WINDOW)).astype(np.float32)
    bias = -_alibi_slopes()[:, None, None] * dist
    return jnp.asarray(bias.reshape(N_KV_HEADS, GROUP * DEC_SEQ, WINDOW + DEC_SEQ))


def _pair_bias():
    r = np.arange(PAIR_BAND)[:, None, None]
    cc = np.arange(PAIR // CHUNK)[None, :, None]
    i = np.arange(CHUNK)[None, None, :]
    j = r - cc * CHUNK
    dist = np.abs(i - (j - WINDOW)).astype(np.float32)
    bias = np.float32(LOG2E) * -_alibi_slopes()[:, None, None, None] * dist[None]
    bias = np.where((j >= 0) & (j < BAND), bias, np.float32(NEG_INF)).astype(np.float32)
    bias = bias.reshape(N_KV_HEADS, GROUP, PAIR_BAND, PAIR).transpose(0, 2, 1, 3)
    return jnp.asarray(bias.reshape(N_KV_HEADS, PAIR_BAND, GROUP * PAIR))


def _sink_lanes(sinks):
    s = LOG2E * sinks.astype(F32).reshape(-1, N_KV_HEADS, GROUP)
    return jnp.repeat(s, PAIR, axis=2)[:, :, None, :]


def _sink_rows(sinks):
    s = sinks.astype(F32).reshape(-1, N_KV_HEADS, GROUP)
    return jnp.repeat(s, DEC_SEQ, axis=2)[:, :, :, None]


def _rows(a):
    return a.astype(F32).reshape(a.shape[0], 1, a.shape[-1])


def _conv_params(w_in, b_in, w_dw, b_dw, ln_g, ln_b, w_out, b_out):
    layers, blocks = w_in.shape[0], D_MODEL // LANES
    w_in = w_in.reshape(layers, D_MODEL, 2, blocks, LANES).swapaxes(2, 3).reshape(layers, D_MODEL, 2 * D_MODEL)
    b_in = b_in.reshape(layers, 2, blocks, LANES).swapaxes(1, 2).reshape(layers, 2 * D_MODEL)
    return (w_in.astype(BF16), _rows(b_in), w_dw.astype(F32), _rows(b_dw), _rows(ln_g), _rows(ln_b),
            w_out.astype(BF16), _rows(b_out))


def kernel(x_prompt, x_sample, cache_k, cache_v, state_conv, norm_ffn1, norm_mix, norm_ffn2, norm_final,
           ffn1_w_in, ffn1_w_out, ffn2_w_in, ffn2_w_out, attn_w_qkv, attn_b_qkv, attn_w_o, attn_sinks,
           conv_w_in, conv_b_in, conv_w_dw, conv_b_dw, conv_ln_g, conv_ln_b, conv_w_out, conv_b_out):
    xs = (x_prompt.reshape(PROMPT_ROWS, D_MODEL), x_sample.reshape(SAMPLE_ROWS, D_MODEL))
    g_ffn1, g_mix, g_ffn2 = _rows(norm_ffn1), _rows(norm_mix), _rows(norm_ffn2)
    g_final = norm_final.astype(F32).reshape(1, D_MODEL)
    attn_w = (attn_w_qkv.astype(BF16), _rows(attn_b_qkv), attn_w_o.astype(BF16))
    bias_p, sinks_p = _pair_bias(), _sink_lanes(attn_sinks)
    bias_s, sinks_s = _sample_bias(), _sink_rows(attn_sinks)
    ck = cache_k.reshape(-1, DEC_BATCH, WINDOW, KV_DIM)
    cv = cache_v.reshape(-1, DEC_BATCH, WINDOW, KV_DIM)
    conv_w = _conv_params(conv_w_in, conv_b_in, conv_w_dw, conv_b_dw, conv_ln_g, conv_ln_b, conv_w_out, conv_b_out)
    hist = jnp.pad(state_conv, ((0, 0), (0, 0), (HIST - CONV_PAD, 0), (0, 0)))

    new_k_p, new_v_p, new_k_s, new_v_s, new_c_p, new_c_s = [], [], [], [], [], []
    for layer in range(DEPTH):
        x = _ffn(xs if layer == 0 else (x,), g_ffn1, ffn1_w_in, ffn1_w_out, layer, g_final, first=layer == 0)
        if layer % 2 == 0:
            a = layer // 2
            x, nk, nv = _attn_prompt(x, g_mix, *attn_w, bias_p, sinks_p, layer, a)
            new_k_p.append(nk)
            new_v_p.append(nv)
            x, nk, nv = _attn_sample(x, ck, cv, g_mix, *attn_w, bias_s, sinks_s, layer, a)
            new_k_s.append(nk)
            new_v_s.append(nv)
        else:
            c = layer // 2
            x, st = _conv_prompt(x, (g_mix,) + conv_w, layer, c)
            new_c_p.append(st[:, HIST - CONV_PAD:])
            x, st = _conv_sample(x, hist, (g_mix,) + conv_w, layer, c)
            new_c_s.append(st[:, HIST - CONV_PAD:])
        x = _ffn((x,), g_ffn2, ffn2_w_in, ffn2_w_out, layer, g_final, last=layer == DEPTH - 1)

    y_prompt, y_sample = x
    kv5 = lambda parts, b: jnp.stack(parts).reshape(len(parts), b, WINDOW, N_KV_HEADS, HEAD_DIM)
    return (y_prompt.reshape(BATCH, SEQ, D_MODEL),
            y_sample.reshape(DEC_BATCH, DEC_SEQ, D_MODEL),
            kv5(new_k_p, BATCH), kv5(new_v_p, BATCH), jnp.stack(new_c_p),
            kv5(new_k_s, DEC_BATCH), kv5(new_v_s, DEC_BATCH), jnp.stack(new_c_s))
```

```python
import functools

import numpy as np
import jax
import jax.numpy as jnp
from jax import lax
from jax.experimental import pallas as pl
from jax.experimental.pallas import tpu as pltpu

D_MODEL = 1024
BATCH = 16
SEQ = 2048
DEPTH = 4
DEC_BATCH = 16
DEC_SEQ = 32
CHUNK = 64
N_HEADS = 16
N_KV_HEADS = 4
HEAD_DIM = 64
GROUP = N_HEADS // N_KV_HEADS
Q_DIM = N_HEADS * HEAD_DIM
KV_DIM = N_KV_HEADS * HEAD_DIM
QKV_DIM = Q_DIM + 2 * KV_DIM
WINDOW = 128
BAND = WINDOW + CHUNK
CONV_WIDTH = 31
CONV_PAD = CONV_WIDTH - 1
FFN_DIM = 2816
NORM_EPS = 1e-5
NEG_INF = -1e30
LOG2E = 1.4426950408889634

F32 = jnp.float32
BF16 = jnp.bfloat16

LANES = 128
PROMPT_ROWS = BATCH * SEQ
SAMPLE_ROWS = DEC_BATCH * DEC_SEQ
ROWS = PROMPT_ROWS + SAMPLE_ROWS
FFN_TILE = 512
FFN_CHUNK = 256
MIX_TILE = 1024
PROMPT_TILES = PROMPT_ROWS // FFN_TILE
SAMPLE_BLOCK = PROMPT_ROWS // SAMPLE_ROWS
PAIR = 2 * CHUNK
PAIR_BAND = WINDOW + PAIR
ONES_ROWS = 16
ATTN_LAG = 1
HIST = 32
CONV_ROWS = 128
VMEM_LIMIT = 56 * 1024 * 1024

assert ROWS % FFN_TILE == 0 and SEQ % MIX_TILE == 0 and MIX_TILE % PAIR == 0
assert PROMPT_ROWS % SAMPLE_ROWS == 0 and SAMPLE_ROWS % FFN_TILE == 0
assert FFN_DIM % FFN_CHUNK == 0 and HIST >= CONV_PAD and DEC_SEQ == HIST


def _rmsnorm(x, g):
    return x * lax.rsqrt(jnp.mean(x * x, axis=-1, keepdims=True) + NORM_EPS) * g


def _dot(a, b):
    return jnp.dot(a, b, preferred_element_type=F32)


def _dot_nt(a, b):
    return lax.dot_general(a, b, (((1,), (1,)), ((), ())), preferred_element_type=F32)


def _const_spec(shape):
    return pl.BlockSpec(shape, lambda *_: (0,) * len(shape), pipeline_mode=pl.Buffered(1))


def _layer_spec(shape, layer):
    return pl.BlockSpec((None,) + shape, lambda *_: (layer,) + (0,) * len(shape), pipeline_mode=pl.Buffered(1))


def _params(dimension_semantics):
    return pltpu.CompilerParams(dimension_semantics=dimension_semantics, vmem_limit_bytes=VMEM_LIMIT)


def _ffn_body(*refs, first, last):
    refs = list(refs)
    i = pl.program_id(0)
    act_ref = refs.pop()
    if first:
        xp_ref, xs_ref = refs.pop(0), refs.pop(0)
        x = jnp.where(i < PROMPT_TILES, xp_ref[...], xs_ref[...])
    else:
        x = refs.pop(0)[...]
    g_ref, win_ref, wout_ref, gfin_ref, *out_refs = refs
    h = _rmsnorm(x, g_ref[...]).astype(BF16)
    for c in range(FFN_DIM // FFN_CHUNK):
        lo = c * FFN_CHUNK
        gate = _dot(h, win_ref[:, lo:lo + FFN_CHUNK].astype(BF16))
        up = _dot(h, win_ref[:, FFN_DIM + lo:FFN_DIM + lo + FFN_CHUNK].astype(BF16))
        act_ref[:, lo:lo + FFN_CHUNK] = (gate * jax.nn.sigmoid(gate) * up).astype(BF16)
    y = x + 0.5 * _dot(act_ref[...], wout_ref[...].astype(BF16))
    if last:
        y = _rmsnorm(y, gfin_ref[...])
        yp_ref, ys_ref = out_refs

        @pl.when(i < PROMPT_TILES)
        def _():
            yp_ref[...] = y

        @pl.when(i >= PROMPT_TILES)
        def _():
            ys_ref[...] = y
    else:
        out_refs[0][...] = y


def _ffn(xs, g, w_in, w_out, layer, g_final, first=False, last=False):
    tile = (FFN_TILE, D_MODEL)
    prompt_spec = pl.BlockSpec(tile, lambda i: (jnp.minimum(i, PROMPT_TILES - 1), 0))
    sample_spec = pl.BlockSpec(tile, lambda i: (jnp.maximum(i - PROMPT_TILES, 0), 0))
    joint_spec = pl.BlockSpec(tile, lambda i: (i, 0))
    split_shape = [jax.ShapeDtypeStruct((PROMPT_ROWS, D_MODEL), F32),
                   jax.ShapeDtypeStruct((SAMPLE_ROWS, D_MODEL), F32)]
    return pl.pallas_call(
        functools.partial(_ffn_body, first=first, last=last),
        grid=(ROWS // FFN_TILE,),
        in_specs=([prompt_spec, sample_spec] if first else [joint_spec]) + [
            _layer_spec((1, D_MODEL), layer),
            _layer_spec((D_MODEL, 2 * FFN_DIM), layer),
            _layer_spec((FFN_DIM, D_MODEL), layer),
            _const_spec((1, D_MODEL)),
        ],
        out_specs=[prompt_spec, sample_spec] if last else joint_spec,
        out_shape=split_shape if last else jax.ShapeDtypeStruct((ROWS, D_MODEL), F32),
        scratch_shapes=[pltpu.VMEM((FFN_TILE, FFN_DIM), BF16)],
        compiler_params=_params(("arbitrary",)),
        name="ffn_first" if first else "ffn_last" if last else "ffn",
    )(*xs, g, w_in, w_out, g_final)


def _stage_q(q, qz_ref, chunk, first_head=0):
    rows = q.shape[0]
    lane = lax.broadcasted_iota(jnp.int32, (rows, LANES), 1)
    for j in range(q.shape[1] // HEAD_DIM):
        kh, g = divmod(first_head + j, GROUP)
        blk = q[:, (j // 2) * LANES:(j // 2 + 1) * LANES]
        if j % 2 != kh % 2:
            blk = pltpu.roll(blk, HEAD_DIM, 1)
        keep = (lane < HEAD_DIM) if kh % 2 == 0 else (lane >= HEAD_DIM)
        blk = jnp.where(keep, blk, 0.0).astype(BF16)
        for c in range(rows // chunk):
            qz_ref[kh, c, g * chunk:(g + 1) * chunk, :] = blk[c * chunk:(c + 1) * chunk]


def _attn_prompt_body(x_ref, g_ref, wqkv_ref, bqkv_ref, wo_ref, bias_ref, sink_ref,
                      o_ref, nk_ref, nv_ref, qz_ref, kp_ref, vt_ref, att_ref):
    t = pl.program_id(1)

    @pl.when(t == 0)
    def _():
        kp_ref[:, 0:WINDOW, :] = jnp.zeros((KV_DIM // LANES, WINDOW, LANES), BF16)
        vt_ref[:, 0:HEAD_DIM, 0:WINDOW] = jnp.zeros((N_KV_HEADS, HEAD_DIM, WINDOW), BF16)
        vt_ref[:, HEAD_DIM:, 0:WINDOW] = jnp.ones((N_KV_HEADS, ONES_ROWS, WINDOW), BF16)

    x = x_ref[...]
    h = _rmsnorm(x, g_ref[...]).astype(BF16)
    kv = _dot(h, wqkv_ref[:, Q_DIM:].astype(BF16)) + bqkv_ref[:, Q_DIM:]
    k = kv[:, :KV_DIM]
    v = kv[:, KV_DIM:]
    nk_ref[...] = k[MIX_TILE - WINDOW:]
    nv_ref[...] = v[MIX_TILE - WINDOW:]
    for p in range(KV_DIM // LANES):
        kp_ref[p, WINDOW:, :] = k[:, p * LANES:(p + 1) * LANES].astype(BF16)
    v_t = v.T
    for kh in range(N_KV_HEADS):
        vt_ref[kh, 0:HEAD_DIM, WINDOW:] = v_t[kh * HEAD_DIM:(kh + 1) * HEAD_DIM, :].astype(BF16)
        vt_ref[kh, HEAD_DIM:, WINDOW:] = jnp.ones((ONES_ROWS, MIX_TILE), BF16)

    key_row = lax.broadcasted_iota(jnp.int32, (PAIR_BAND, GROUP * PAIR), 0)
    pairs = MIX_TILE // PAIR
    units = [(pp, kh) for kh in range(N_KV_HEADS) for pp in range(pairs)]
    keys = lambda pp: slice(pp * PAIR, pp * PAIR + PAIR_BAND)
    group_cols = lambda kh: slice(kh * GROUP * HEAD_DIM, (kh + 2) * GROUP * HEAD_DIM)

    def project_q(kh):
        q = _dot(h, wqkv_ref[:, group_cols(kh)].astype(BF16)) + bqkv_ref[:, group_cols(kh)]
        _stage_q(q * (LOG2E * HEAD_DIM ** -0.5), qz_ref, PAIR, first_head=kh * GROUP)

    def project_out(kh):
        return lax.dot_general(att_ref[group_cols(kh), :], wo_ref[group_cols(kh), :].astype(BF16),
                               (((0,), (0,)), ((), ())), preferred_element_type=F32)

    def scores(pp, kh):
        s = _dot_nt(kp_ref[kh // 2, keys(pp), :], qz_ref[kh, pp]) + bias_ref[kh]
        if pp == 0:
            s = jnp.where(key_row >= WINDOW - t * MIX_TILE, s, NEG_INF)
        return s, jnp.maximum(jnp.max(s, axis=0, keepdims=True), sink_ref[kh])

    def finish(pp, kh, p, sink_p):
        o = _dot(vt_ref[kh, :, keys(pp)], p)
        o = o[0:HEAD_DIM, :] * (1.0 / (o[HEAD_DIM:HEAD_DIM + 1, :] + sink_p))
        for g in range(GROUP):
            head = kh * GROUP + g
            att_ref[head * HEAD_DIM:(head + 1) * HEAD_DIM, pp * PAIR:(pp + 1) * PAIR] = (
                o[:, g * PAIR:(g + 1) * PAIR].astype(BF16))

    scored, exped = {}, {}
    for step in range(len(units) + 2 * ATTN_LAG):
        if step < len(units):
            pp, kh = units[step]
            if pp == 0 and kh % 2 == 0:
                project_q(kh)
            scored[step] = scores(pp, kh)
        u = step - ATTN_LAG
        if 0 <= u < len(units):
            s, m = scored.pop(u)
            exped[u] = jnp.exp2(s - m).astype(BF16), jnp.exp2(sink_ref[units[u][1]] - m)
        u = step - 2 * ATTN_LAG
        if u >= 0:
            pp, kh = units[u]
            finish(pp, kh, *exped.pop(u))
            if pp == pairs - 1 and kh == 1:
                o_ref[...] = x + project_out(0)
            elif pp == pairs - 1 and kh == 3:
                o_ref[...] += project_out(2)

    kp_ref[:, 0:WINDOW, :] = kp_ref[:, MIX_TILE:MIX_TILE + WINDOW, :]
    vt_ref[:, :, 0:WINDOW] = vt_ref[:, :, MIX_TILE:MIX_TILE + WINDOW]


def _attn_prompt(x, g, w_qkv, b_qkv, w_o, bias, sinks, layer, a):
    tiles = SEQ // MIX_TILE
    kv_shape = jax.ShapeDtypeStruct((BATCH, WINDOW, KV_DIM), F32)
    kv_spec = pl.BlockSpec((None, WINDOW, KV_DIM), lambda b, t: (b, 0, 0))
    return pl.pallas_call(
        _attn_prompt_body,
        grid=(BATCH, tiles),
        in_specs=[
            pl.BlockSpec((MIX_TILE, D_MODEL), lambda b, t: (b * tiles + t, 0)),
            _layer_spec((1, D_MODEL), layer),
            _layer_spec((D_MODEL, QKV_DIM), a),
            _layer_spec((1, QKV_DIM), a),
            _layer_spec((Q_DIM, D_MODEL), a),
            _const_spec((N_KV_HEADS, PAIR_BAND, GROUP * PAIR)),
            _layer_spec((N_KV_HEADS, 1, GROUP * PAIR), a),
        ],
        out_specs=[
            pl.BlockSpec((MIX_TILE, D_MODEL), lambda b, t: (b * tiles + t, 0)),
            kv_spec, kv_spec,
        ],
        out_shape=[jax.ShapeDtypeStruct((ROWS, D_MODEL), F32), kv_shape, kv_shape],
        scratch_shapes=[
            pltpu.VMEM((N_KV_HEADS, MIX_TILE // PAIR, GROUP * PAIR, LANES), BF16),
            pltpu.VMEM((KV_DIM // LANES, WINDOW + MIX_TILE, LANES), BF16),
            pltpu.VMEM((N_KV_HEADS, HEAD_DIM + ONES_ROWS, WINDOW + MIX_TILE), BF16),
            pltpu.VMEM((Q_DIM, MIX_TILE), BF16),
        ],
        input_output_aliases={0: 0},
        compiler_params=_params(("arbitrary", "arbitrary")),
        name="attn_prompt",
    )(x, g, w_qkv, b_qkv, w_o, bias, sinks)


def _stage_kv(k, v, kp_ref, v2_ref, row0):
    rows = k.shape[0]
    lane = lax.broadcasted_iota(jnp.int32, (rows, LANES), 1)
    for p in range(KV_DIM // LANES):
        cols = slice(p * LANES, (p + 1) * LANES)
        kp_ref[p, row0:row0 + rows, :] = k[:, cols].astype(BF16)
        pair = v[:, cols]
        lo_even = jnp.where(lane < HEAD_DIM, pair, 0.0)
        hi_odd = jnp.where(lane >= HEAD_DIM, pair, 0.0)
        hi_even = pltpu.roll(lo_even, HEAD_DIM, 1)
        lo_odd = pltpu.roll(hi_odd, HEAD_DIM, 1)
        for kh, lo, hi in ((2 * p, lo_even, hi_even), (2 * p + 1, lo_odd, hi_odd)):
            v2_ref[kh, row0:row0 + rows, 0:LANES] = lo.astype(BF16)
            v2_ref[kh, row0:row0 + rows, LANES:2 * LANES] = hi.astype(BF16)


def _attend_rows(qz, kband, vband, bias, sink):
    t = qz.shape[0] // GROUP
    s = _dot_nt(qz, kband) + bias
    m = jnp.maximum(jnp.max(s, axis=-1, keepdims=True), sink)
    p = jnp.exp(s - m).astype(BF16)
    o2 = _dot(p, vband)
    inv = 1.0 / (_dot(p, jnp.ones((p.shape[1], LANES), BF16)) + jnp.exp(sink - m))
    o2 = o2 * jnp.concatenate([inv, inv], axis=1)
    return (o2[0:t, 0:LANES] + o2[t:2 * t, LANES:2 * LANES],
            o2[2 * t:3 * t, 0:LANES] + o2[3 * t:4 * t, LANES:2 * LANES])


def _attn_sample_body(x_ref, ck_ref, cv_ref, g_ref, wqkv_ref, bqkv_ref, wo_ref, bias_ref, sink_ref,
                      o_ref, nk_ref, nv_ref, qz_ref, kp_ref, v2_ref, att_ref):
    x = x_ref[...]
    h = _rmsnorm(x, g_ref[...]).astype(BF16)
    qkv = _dot(h, wqkv_ref[...].astype(BF16)) + bqkv_ref[...]
    q = qkv[:, :Q_DIM] * (HEAD_DIM ** -0.5)
    k = qkv[:, Q_DIM:Q_DIM + KV_DIM]
    v = qkv[:, Q_DIM + KV_DIM:]
    for b in range(DEC_BATCH):
        rows = slice(b * DEC_SEQ, (b + 1) * DEC_SEQ)
        ck = ck_ref[b]
        cv = cv_ref[b]
        nk_ref[b, 0:WINDOW - DEC_SEQ, :] = ck[DEC_SEQ:]
        nk_ref[b, WINDOW - DEC_SEQ:, :] = k[rows]
        nv_ref[b, 0:WINDOW - DEC_SEQ, :] = cv[DEC_SEQ:]
        nv_ref[b, WINDOW - DEC_SEQ:, :] = v[rows]
        _stage_q(q[rows], qz_ref.at[b], DEC_SEQ)
        _stage_kv(ck, cv, kp_ref.at[b], v2_ref.at[b], 0)
        _stage_kv(k[rows], v[rows], kp_ref.at[b], v2_ref.at[b], WINDOW)
        pairs = []
        for kh in range(N_KV_HEADS):
            pairs.extend(_attend_rows(qz_ref[b, kh, 0], kp_ref[b, kh // 2], v2_ref[b, kh], bias_ref[kh], sink_ref[kh]))
        att_ref[rows, :] = jnp.concatenate(pairs, axis=1).astype(BF16)
    o_ref[...] = x + _dot(att_ref[...], wo_ref[...].astype(BF16))


def _attn_sample(x, ck, cv, g, w_qkv, b_qkv, w_o, bias, sinks, layer, a):
    nkeys = WINDOW + DEC_SEQ
    kv_shape = jax.ShapeDtypeStruct((DEC_BATCH, WINDOW, KV_DIM), F32)
    kv_spec = pl.BlockSpec((DEC_BATCH, WINDOW, KV_DIM), lambda i: (0, 0, 0))
    x_spec = pl.BlockSpec((SAMPLE_ROWS, D_MODEL), lambda i: (SAMPLE_BLOCK, 0))
    return pl.pallas_call(
        _attn_sample_body,
        grid=(1,),
        in_specs=[
            x_spec,
            _layer_spec((DEC_BATCH, WINDOW, KV_DIM), a),
            _layer_spec((DEC_BATCH, WINDOW, KV_DIM), a),
            _layer_spec((1, D_MODEL), layer),
            _layer_spec((D_MODEL, QKV_DIM), a),
            _layer_spec((1, QKV_DIM), a),
            _layer_spec((Q_DIM, D_MODEL), a),
            _const_spec((N_KV_HEADS, GROUP * DEC_SEQ, nkeys)),
            _layer_spec((N_KV_HEADS, GROUP * DEC_SEQ, 1), a),
        ],
        out_specs=[x_spec, kv_spec, kv_spec],
        out_shape=[jax.ShapeDtypeStruct((ROWS, D_MODEL), F32), kv_shape, kv_shape],
        scratch_shapes=[
            pltpu.VMEM((DEC_BATCH, N_KV_HEADS, 1, GROUP * DEC_SEQ, LANES), BF16),
            pltpu.VMEM((DEC_BATCH, KV_DIM // LANES, nkeys, LANES), BF16),
            pltpu.VMEM((DEC_BATCH, N_KV_HEADS, nkeys, 2 * LANES), BF16),
            pltpu.VMEM((SAMPLE_ROWS, Q_DIM), BF16),
        ],
        input_output_aliases={0: 0},
        compiler_params=_params(("arbitrary",)),
        name="attn_sample",
    )(x, ck, cv, g, w_qkv, b_qkv, w_o, bias, sinks)


def _glu_cols(h, c, win_ref, bin_ref):
    cols = slice(c * 2 * LANES, (c + 1) * 2 * LANES)
    ag = _dot(h, win_ref[:, cols]) + bin_ref[:, cols]
    return ag[:, :LANES] * jax.nn.sigmoid(ag[:, LANES:])


def _conv_cols(ubuf_ref, c, src0, dst0, rb, wdw_ref, bdw_ref, conv_ref):
    cols = slice(c * LANES, (c + 1) * LANES)
    acc = jnp.broadcast_to(bdw_ref[:, cols], (rb, LANES))
    for w in range(CONV_WIDTH):
        tap0 = src0 + HIST - CONV_PAD + w
        acc = acc + ubuf_ref[c, tap0:tap0 + rb, :] * wdw_ref[w:w + 1, cols]
    conv_ref[dst0:dst0 + rb, cols] = acc


def _ln_silu(r0, rb, lng_ref, lnb_ref, conv_ref, y_ref):
    yc = conv_ref[r0:r0 + rb, :]
    xc = yc - jnp.mean(yc, axis=-1, keepdims=True)
    yn = xc * lax.rsqrt(jnp.mean(xc * xc, axis=-1, keepdims=True) + NORM_EPS)
    yn = yn * lng_ref[...] + lnb_ref[...]
    y_ref[r0:r0 + rb, :] = (yn * jax.nn.sigmoid(yn)).astype(BF16)


def _conv_prompt_body(x_ref, g_ref, win_ref, bin_ref, wdw_ref, bdw_ref, lng_ref, lnb_ref, wout_ref,
                      bout_ref, o_ref, st_ref, ubuf_ref, conv_ref, y_ref):
    @pl.when(pl.program_id(1) == 0)
    def _():
        ubuf_ref[:, 0:HIST, :] = jnp.zeros((D_MODEL // LANES, HIST, LANES), F32)

    x = x_ref[...]
    h = _rmsnorm(x, g_ref[...]).astype(BF16)
    for c in range(D_MODEL // LANES):
        u = _glu_cols(h, c, win_ref, bin_ref)
        st_ref[:, c * LANES:(c + 1) * LANES] = u[MIX_TILE - HIST:]
        ubuf_ref[c, HIST:, :] = u
        for r0 in range(0, MIX_TILE, CONV_ROWS):
            _conv_cols(ubuf_ref, c, r0, r0, CONV_ROWS, wdw_ref, bdw_ref, conv_ref)
    half = MIX_TILE // 2
    for r in (0, half):
        for r0 in range(r, r + half, CONV_ROWS):
            _ln_silu(r0, CONV_ROWS, lng_ref, lnb_ref, conv_ref, y_ref)
        o_ref[r:r + half, :] = x[r:r + half] + _dot(y_ref[r:r + half, :], wout_ref[...]) + bout_ref[...]
    ubuf_ref[:, 0:HIST, :] = ubuf_ref[:, MIX_TILE:MIX_TILE + HIST, :]


def _conv_sample_body(x_ref, hist_ref, g_ref, win_ref, bin_ref, wdw_ref, bdw_ref, lng_ref, lnb_ref,
                      wout_ref, bout_ref, o_ref, st_ref, ubuf_ref, conv_ref, y_ref):
    x = x_ref[...]
    h = _rmsnorm(x, g_ref[...]).astype(BF16)
    for c in range(D_MODEL // LANES):
        cols = slice(c * LANES, (c + 1) * LANES)
        u = _glu_cols(h, c, win_ref, bin_ref)
        for b in range(DEC_BATCH):
            rows = slice(b * DEC_SEQ, (b + 1) * DEC_SEQ)
            st_ref[b, :, cols] = u[rows]
            ubuf_ref[b, c, 0:HIST, :] = hist_ref[b, :, cols]
            ubuf_ref[b, c, HIST:, :] = u[rows]
            _conv_cols(ubuf_ref.at[b], c, 0, b * DEC_SEQ, DEC_SEQ, wdw_ref, bdw_ref, conv_ref)
    _ln_silu(0, SAMPLE_ROWS, lng_ref, lnb_ref, conv_ref, y_ref)
    o_ref[...] = x + _dot(y_ref[...], wout_ref[...]) + bout_ref[...]


def _conv_weight_specs(layer, c):
    return [
        _layer_spec((1, D_MODEL), layer),
        _layer_spec((D_MODEL, 2 * D_MODEL), c),
        _layer_spec((1, 2 * D_MODEL), c),
        _layer_spec((CONV_WIDTH, D_MODEL), c),
        _layer_spec((1, D_MODEL), c),
        _layer_spec((1, D_MODEL), c),
        _layer_spec((1, D_MODEL), c),
        _layer_spec((D_MODEL, D_MODEL), c),
        _layer_spec((1, D_MODEL), c),
    ]


def _conv_prompt(x, weights, layer, c):
    tiles = SEQ // MIX_TILE
    return pl.pallas_call(
        _conv_prompt_body,
        grid=(BATCH, tiles),
        in_specs=[pl.BlockSpec((MIX_TILE, D_MODEL), lambda b, t: (b * tiles + t, 0))] + _conv_weight_specs(layer, c),
        out_specs=[
            pl.BlockSpec((MIX_TILE, D_MODEL), lambda b, t: (b * tiles + t, 0)),
            pl.BlockSpec((None, HIST, D_MODEL), lambda b, t: (b, 0, 0)),
        ],
        out_shape=[jax.ShapeDtypeStruct((ROWS, D_MODEL), F32),
                   jax.ShapeDtypeStruct((BATCH, HIST, D_MODEL), F32)],
        scratch_shapes=[
            pltpu.VMEM((D_MODEL // LANES, HIST + MIX_TILE, LANES), F32),
            pltpu.VMEM((MIX_TILE, D_MODEL), F32),
            pltpu.VMEM((MIX_TILE, D_MODEL), BF16),
        ],
        input_output_aliases={0: 0},
        compiler_params=_params(("arbitrary", "arbitrary")),
        name="conv_prompt",
    )(x, *weights)


def _conv_sample(x, hist, weights, layer, c):
    x_spec = pl.BlockSpec((SAMPLE_ROWS, D_MODEL), lambda i: (SAMPLE_BLOCK, 0))
    return pl.pallas_call(
        _conv_sample_body,
        grid=(1,),
        in_specs=[x_spec, _layer_spec((DEC_BATCH, HIST, D_MODEL), c)] + _conv_weight_specs(layer, c),
        out_specs=[x_spec, pl.BlockSpec((DEC_BATCH, HIST, D_MODEL), lambda i: (0, 0, 0))],
        out_shape=[jax.ShapeDtypeStruct((ROWS, D_MODEL), F32),
                   jax.ShapeDtypeStruct((DEC_BATCH, HIST, D_MODEL), F32)],
        scratch_shapes=[
            pltpu.VMEM((DEC_BATCH, D_MODEL // LANES, HIST + DEC_SEQ, LANES), F32),
            pltpu.VMEM((SAMPLE_ROWS, D_MODEL), F32),
            pltpu.VMEM((SAMPLE_ROWS, D_MODEL), BF16),
        ],
        input_output_aliases={0: 0},
        compiler_params=_params(("arbitrary",)),
        name="conv_sample",
    )(x, hist, *weights)


def _alibi_slopes():
    return np.array([2.0 ** (-8.0 * (h + 1) / N_HEADS) for h in range(N_HEADS)], dtype=np.float32)


def _sample_bias():
    i = np.arange(DEC_SEQ)[:, None]
    j = np.arange(WINDOW + DEC_SEQ)[None, :]
    dist = np.abs(i - (j - WINDOW)).astype(np.float32)
    bias = -_alibi_slopes()[:, None, None] * dist
    return jnp.asarray(bias.reshape(N_KV_HEADS, GROUP * DEC_SEQ, WINDOW + DEC_SEQ))


def _pair_bias():
    r = np.arange(PAIR_BAND)[:, None, None]
    cc = np.arange(PAIR // CHUNK)[None, :, None]
    i = np.arange(CHUNK)[None, None, :]
    j = r - cc * CHUNK
    dist = np.abs(i - (j - WINDOW)).astype(np.float32)
    bias = np.float32(LOG2E) * -_alibi_slopes()[:, None, None, None] * dist[None]
    bias = np.where((j >= 0) & (j < BAND), bias, np.float32(NEG_INF)).astype(np.float32)
    bias = bias.reshape(N_KV_HEADS, GROUP, PAIR_BAND, PAIR).transpose(0, 2, 1, 3)
    return jnp.asarray(bias.reshape(N_KV_HEADS, PAIR_BAND, GROUP * PAIR))


def _sink_lanes(sinks):
    s = LOG2E * sinks.astype(F32).reshape(-1, N_KV_HEADS, GROUP)
    return jnp.repeat(s, PAIR, axis=2)[:, :, None, :]


def _sink_rows(sinks):
    s = sinks.astype(F32).reshape(-1, N_KV_HEADS, GROUP)
    return jnp.repeat(s, DEC_SEQ, axis=2)[:, :, :, None]


def _rows(a):
    return a.astype(F32).reshape(a.shape[0], 1, a.shape[-1])


def _conv_params(w_in, b_in, w_dw, b_dw, ln_g, ln_b, w_out, b_out):
    layers, blocks = w_in.shape[0], D_MODEL // LANES
    w_in = w_in.astype(BF16).reshape(layers, D_MODEL, 2, blocks, LANES).swapaxes(2, 3).reshape(layers, D_MODEL, 2 * D_MODEL)
    b_in = b_in.reshape(layers, 2, blocks, LANES).swapaxes(1, 2).reshape(layers, 2 * D_MODEL)
    return (w_in, _rows(b_in), w_dw.astype(F32), _rows(b_dw), _rows(ln_g), _rows(ln_b),
            w_out.astype(BF16), _rows(b_out))


def kernel(x_prompt, x_sample, cache_k, cache_v, state_conv, norm_ffn1, norm_mix, norm_ffn2, norm_final,
           ffn1_w_in, ffn1_w_out, ffn2_w_in, ffn2_w_out, attn_w_qkv, attn_b_qkv, attn_w_o, attn_sinks,
           conv_w_in, conv_b_in, conv_w_dw, conv_b_dw, conv_ln_g, conv_ln_b, conv_w_out, conv_b_out):
    xs = (x_prompt.reshape(PROMPT_ROWS, D_MODEL), x_sample.reshape(SAMPLE_ROWS, D_MODEL))
    g_ffn1, g_mix, g_ffn2 = _rows(norm_ffn1), _rows(norm_mix), _rows(norm_ffn2)
    g_final = norm_final.astype(F32).reshape(1, D_MODEL)
    attn_w = (attn_w_qkv.astype(F32), _rows(attn_b_qkv), attn_w_o.astype(F32))
    bias_p, sinks_p = _pair_bias(), _sink_lanes(attn_sinks)
    bias_s, sinks_s = _sample_bias(), _sink_rows(attn_sinks)
    ck = cache_k.reshape(-1, DEC_BATCH, WINDOW, KV_DIM)
    cv = cache_v.reshape(-1, DEC_BATCH, WINDOW, KV_DIM)
    conv_w = _conv_params(conv_w_in, conv_b_in, conv_w_dw, conv_b_dw, conv_ln_g, conv_ln_b, conv_w_out, conv_b_out)
    hist = jnp.pad(state_conv, ((0, 0), (0, 0), (HIST - CONV_PAD, 0), (0, 0)))

    new_k_p, new_v_p, new_k_s, new_v_s, new_c_p, new_c_s = [], [], [], [], [], []
    for layer in range(DEPTH):
        x = _ffn(xs if layer == 0 else (x,), g_ffn1, ffn1_w_in, ffn1_w_out, layer, g_final, first=layer == 0)
        if layer % 2 == 0:
            a = layer // 2
            x, nk, nv = _attn_prompt(x, g_mix, *attn_w, bias_p, sinks_p, layer, a)
            new_k_p.append(nk)
            new_v_p.append(nv)
            x, nk, nv = _attn_sample(x, ck, cv, g_mix, *attn_w, bias_s, sinks_s, layer, a)
            new_k_s.append(nk)
            new_v_s.append(nv)
        else:
            c = layer // 2
            x, st = _conv_prompt(x, (g_mix,) + conv_w, layer, c)
            new_c_p.append(st[:, HIST - CONV_PAD:])
            x, st = _conv_sample(x, hist, (g_mix,) + conv_w, layer, c)
            new_c_s.append(st[:, HIST - CONV_PAD:])
        x = _ffn((x,), g_ffn2, ffn2_w_in, ffn2_w_out, layer, g_final, last=layer == DEPTH - 1)

    y_prompt, y_sample = x
    kv5 = lambda parts, b: jnp.stack(parts).reshape(len(parts), b, WINDOW, N_KV_HEADS, HEAD_DIM)
    return (y_prompt.reshape(BATCH, SEQ, D_MODEL),
            y_sample.reshape(DEC_BATCH, DEC_SEQ, D_MODEL),
            kv5(new_k_p, BATCH), kv5(new_v_p, BATCH), jnp.stack(new_c_p),
            kv5(new_k_s, DEC_BATCH), kv5(new_v_s, DEC_BATCH), jnp.stack(new_c_s))
```

```python
import functools

import numpy as np
import jax
import jax.numpy as jnp
from jax import lax
from jax.experimental import pallas as pl
from jax.experimental.pallas import tpu as pltpu

D_MODEL = 1024
BATCH = 16
SEQ = 2048
DEPTH = 4
DEC_BATCH = 16
DEC_SEQ = 32
CHUNK = 64
N_HEADS = 16
N_KV_HEADS = 4
HEAD_DIM = 64
GROUP = N_HEADS // N_KV_HEADS
Q_DIM = N_HEADS * HEAD_DIM
KV_DIM = N_KV_HEADS * HEAD_DIM
QKV_DIM = Q_DIM + 2 * KV_DIM
WINDOW = 128
BAND = WINDOW + CHUNK
CONV_WIDTH = 31
CONV_PAD = CONV_WIDTH - 1
FFN_DIM = 2816
NORM_EPS = 1e-5
NEG_INF = -1e30
LOG2E = 1.4426950408889634

F32 = jnp.float32
BF16 = jnp.bfloat16

LANES = 128
PROMPT_ROWS = BATCH * SEQ
SAMPLE_ROWS = DEC_BATCH * DEC_SEQ
ROWS = PROMPT_ROWS + SAMPLE_ROWS
FFN_TILE = 512
FFN_CHUNK = 256
MIX_TILE = 1024
PROMPT_TILES = PROMPT_ROWS // FFN_TILE
SAMPLE_BLOCK = PROMPT_ROWS // SAMPLE_ROWS
PAIR = 2 * CHUNK
PAIR_BAND = WINDOW + PAIR
ONES_ROWS = 16
ATTN_LAG = 1
HIST = 32
CONV_ROWS = 128
VMEM_LIMIT = 56 * 1024 * 1024

assert ROWS % FFN_TILE == 0 and SEQ % MIX_TILE == 0 and MIX_TILE % PAIR == 0
assert PROMPT_ROWS % SAMPLE_ROWS == 0 and SAMPLE_ROWS % FFN_TILE == 0
assert FFN_DIM % FFN_CHUNK == 0 and HIST >= CONV_PAD and DEC_SEQ == HIST


def _rmsnorm(x, g):
    return x * lax.rsqrt(jnp.mean(x * x, axis=-1, keepdims=True) + NORM_EPS) * g


def _dot(a, b):
    return jnp.dot(a, b, preferred_element_type=F32)


def _dot_nt(a, b):
    return lax.dot_general(a, b, (((1,), (1,)), ((), ())), preferred_element_type=F32)


def _const_spec(shape):
    return pl.BlockSpec(shape, lambda *_: (0,) * len(shape), pipeline_mode=pl.Buffered(1))


def _layer_spec(shape, layer):
    return pl.BlockSpec((None,) + shape, lambda *_: (layer,) + (0,) * len(shape), pipeline_mode=pl.Buffered(1))


def _params(dimension_semantics):
    return pltpu.CompilerParams(dimension_semantics=dimension_semantics, vmem_limit_bytes=VMEM_LIMIT)


def _ffn_body(*refs, first, last):
    refs = list(refs)
    i = pl.program_id(0)
    act_ref = refs.pop()
    if first:
        xp_ref, xs_ref = refs.pop(0), refs.pop(0)
        x = jnp.where(i < PROMPT_TILES, xp_ref[...], xs_ref[...])
    else:
        x = refs.pop(0)[...]
    g_ref, win_ref, wout_ref, gfin_ref, *out_refs = refs
    h = _rmsnorm(x, g_ref[...]).astype(BF16)
    for c in range(FFN_DIM // FFN_CHUNK):
        lo = c * FFN_CHUNK
        gate = _dot(h, win_ref[:, lo:lo + FFN_CHUNK].astype(BF16))
        up = _dot(h, win_ref[:, FFN_DIM + lo:FFN_DIM + lo + FFN_CHUNK].astype(BF16))
        act_ref[:, lo:lo + FFN_CHUNK] = (gate * jax.nn.sigmoid(gate) * up).astype(BF16)
    y = x + 0.5 * _dot(act_ref[...], wout_ref[...].astype(BF16))
    if last:
        y = _rmsnorm(y, gfin_ref[...])
        yp_ref, ys_ref = out_refs

        @pl.when(i < PROMPT_TILES)
        def _():
            yp_ref[...] = y

        @pl.when(i >= PROMPT_TILES)
        def _():
            ys_ref[...] = y
    else:
        out_refs[0][...] = y


def _ffn(xs, g, w_in, w_out, layer, g_final, first=False, last=False):
    tile = (FFN_TILE, D_MODEL)
    prompt_spec = pl.BlockSpec(tile, lambda i: (jnp.minimum(i, PROMPT_TILES - 1), 0))
    sample_spec = pl.BlockSpec(tile, lambda i: (jnp.maximum(i - PROMPT_TILES, 0), 0))
    joint_spec = pl.BlockSpec(tile, lambda i: (i, 0))
    split_shape = [jax.ShapeDtypeStruct((PROMPT_ROWS, D_MODEL), F32),
                   jax.ShapeDtypeStruct((SAMPLE_ROWS, D_MODEL), F32)]
    return pl.pallas_call(
        functools.partial(_ffn_body, first=first, last=last),
        grid=(ROWS // FFN_TILE,),
        in_specs=([prompt_spec, sample_spec] if first else [joint_spec]) + [
            _layer_spec((1, D_MODEL), layer),
            _layer_spec((D_MODEL, 2 * FFN_DIM), layer),
            _layer_spec((FFN_DIM, D_MODEL), layer),
            _const_spec((1, D_MODEL)),
        ],
        out_specs=[prompt_spec, sample_spec] if last else joint_spec,
        out_shape=split_shape if last else jax.ShapeDtypeStruct((ROWS, D_MODEL), F32),
        scratch_shapes=[pltpu.VMEM((FFN_TILE, FFN_DIM), BF16)],
        compiler_params=_params(("arbitrary",)),
        name="ffn_first" if first else "ffn_last" if last else "ffn",
    )(*xs, g, w_in, w_out, g_final)


def _stage_q(q, qz_ref, chunk, first_head=0):
    rows = q.shape[0]
    lane = lax.broadcasted_iota(jnp.int32, (rows, LANES), 1)
    for j in range(q.shape[1] // HEAD_DIM):
        kh, g = divmod(first_head + j, GROUP)
        blk = q[:, (j // 2) * LANES:(j // 2 + 1) * LANES]
        if j % 2 != kh % 2:
            blk = pltpu.roll(blk, HEAD_DIM, 1)
        keep = (lane < HEAD_DIM) if kh % 2 == 0 else (lane >= HEAD_DIM)
        blk = jnp.where(keep, blk, 0.0).astype(BF16)
        for c in range(rows // chunk):
            qz_ref[kh, c, g * chunk:(g + 1) * chunk, :] = blk[c * chunk:(c + 1) * chunk]


def _attn_prompt_body(x_ref, g_ref, wqkv_ref, bqkv_ref, wo_ref, bias_ref, sink_ref,
                      o_ref, nk_ref, nv_ref, qz_ref, kp_ref, vt_ref, att_ref):
    t = pl.program_id(1)

    @pl.when(t == 0)
    def _():
        kp_ref[:, 0:WINDOW, :] = jnp.zeros((KV_DIM // LANES, WINDOW, LANES), BF16)
        vt_ref[:, 0:HEAD_DIM, 0:WINDOW] = jnp.zeros((N_KV_HEADS, HEAD_DIM, WINDOW), BF16)
        vt_ref[:, HEAD_DIM:, 0:WINDOW] = jnp.ones((N_KV_HEADS, ONES_ROWS, WINDOW), BF16)

    x = x_ref[...]
    h = _rmsnorm(x, g_ref[...]).astype(BF16)
    qkv = _dot(h, wqkv_ref[...]) + bqkv_ref[...]
    k = qkv[:, Q_DIM:Q_DIM + KV_DIM]
    v = qkv[:, Q_DIM + KV_DIM:]
    nk_ref[...] = k[MIX_TILE - WINDOW:]
    nv_ref[...] = v[MIX_TILE - WINDOW:]
    _stage_q(qkv[:, :Q_DIM] * (LOG2E * HEAD_DIM ** -0.5), qz_ref, PAIR)
    for p in range(KV_DIM // LANES):
        kp_ref[p, WINDOW:, :] = k[:, p * LANES:(p + 1) * LANES].astype(BF16)
    v_t = v.T
    for kh in range(N_KV_HEADS):
        vt_ref[kh, 0:HEAD_DIM, WINDOW:] = v_t[kh * HEAD_DIM:(kh + 1) * HEAD_DIM, :].astype(BF16)
        vt_ref[kh, HEAD_DIM:, WINDOW:] = jnp.ones((ONES_ROWS, MIX_TILE), BF16)

    key_row = lax.broadcasted_iota(jnp.int32, (PAIR_BAND, GROUP * PAIR), 0)
    units = [(pp, kh) for pp in range(MIX_TILE // PAIR) for kh in range(N_KV_HEADS)]
    keys = lambda pp: slice(pp * PAIR, pp * PAIR + PAIR_BAND)

    def scores(pp, kh):
        s = _dot_nt(kp_ref[kh // 2, keys(pp), :], qz_ref[kh, pp]) + bias_ref[kh]
        if pp == 0:
            s = jnp.where(key_row >= WINDOW - t * MIX_TILE, s, NEG_INF)
        return s, jnp.maximum(jnp.max(s, axis=0, keepdims=True), sink_ref[kh])

    def finish(pp, kh, p, sink_p):
        o = _dot(vt_ref[kh, :, keys(pp)], p)
        o = o[0:HEAD_DIM, :] * (1.0 / (o[HEAD_DIM:HEAD_DIM + 1, :] + sink_p))
        for g in range(GROUP):
            head = kh * GROUP + g
            att_ref[head * HEAD_DIM:(head + 1) * HEAD_DIM, pp * PAIR:(pp + 1) * PAIR] = (
                o[:, g * PAIR:(g + 1) * PAIR].astype(BF16))

    scored, exped = {}, {}
    for step in range(len(units) + 2 * ATTN_LAG):
        if step < len(units):
            scored[step] = scores(*units[step])
        u = step - ATTN_LAG
        if 0 <= u < len(units):
            s, m = scored.pop(u)
            exped[u] = jnp.exp2(s - m).astype(BF16), jnp.exp2(sink_ref[units[u][1]] - m)
        u = step - 2 * ATTN_LAG
        if u >= 0:
            finish(*units[u], *exped.pop(u))

    out = lax.dot_general(att_ref[...], wo_ref[...], (((0,), (0,)), ((), ())), preferred_element_type=F32)
    o_ref[...] = x + out
    kp_ref[:, 0:WINDOW, :] = kp_ref[:, MIX_TILE:MIX_TILE + WINDOW, :]
    vt_ref[:, :, 0:WINDOW] = vt_ref[:, :, MIX_TILE:MIX_TILE + WINDOW]


def _attn_prompt(x, g, w_qkv, b_qkv, w_o, bias, sinks, layer, a):
    tiles = SEQ // MIX_TILE
    kv_shape = jax.ShapeDtypeStruct((BATCH, WINDOW, KV_DIM), F32)
    kv_spec = pl.BlockSpec((None, WINDOW, KV_DIM), lambda b, t: (b, 0, 0))
    return pl.pallas_call(
        _attn_prompt_body,
        grid=(BATCH, tiles),
        in_specs=[
            pl.BlockSpec((MIX_TILE, D_MODEL), lambda b, t: (b * tiles + t, 0)),
            _layer_spec((1, D_MODEL), layer),
            _layer_spec((D_MODEL, QKV_DIM), a),
            _layer_spec((1, QKV_DIM), a),
            _layer_spec((Q_DIM, D_MODEL), a),
            _const_spec((N_KV_HEADS, PAIR_BAND, GROUP * PAIR)),
            _layer_spec((N_KV_HEADS, 1, GROUP * PAIR), a),
        ],
        out_specs=[
            pl.BlockSpec((MIX_TILE, D_MODEL), lambda b, t: (b * tiles + t, 0)),
            kv_spec, kv_spec,
        ],
        out_shape=[jax.ShapeDtypeStruct((ROWS, D_MODEL), F32), kv_shape, kv_shape],
        scratch_shapes=[
            pltpu.VMEM((N_KV_HEADS, MIX_TILE // PAIR, GROUP * PAIR, LANES), BF16),
            pltpu.VMEM((KV_DIM // LANES, WINDOW + MIX_TILE, LANES), BF16),
            pltpu.VMEM((N_KV_HEADS, HEAD_DIM + ONES_ROWS, WINDOW + MIX_TILE), BF16),
            pltpu.VMEM((Q_DIM, MIX_TILE), BF16),
        ],
        input_output_aliases={0: 0},
        compiler_params=_params(("arbitrary", "arbitrary")),
        name="attn_prompt",
    )(x, g, w_qkv, b_qkv, w_o, bias, sinks)


def _stage_kv(k, v, kp_ref, v2_ref, row0):
    rows = k.shape[0]
    lane = lax.broadcasted_iota(jnp.int32, (rows, LANES), 1)
    for p in range(KV_DIM // LANES):
        cols = slice(p * LANES, (p + 1) * LANES)
        kp_ref[p, row0:row0 + rows, :] = k[:, cols].astype(BF16)
        pair = v[:, cols]
        lo_even = jnp.where(lane < HEAD_DIM, pair, 0.0)
        hi_odd = jnp.where(lane >= HEAD_DIM, pair, 0.0)
        hi_even = pltpu.roll(lo_even, HEAD_DIM, 1)
        lo_odd = pltpu.roll(hi_odd, HEAD_DIM, 1)
        for kh, lo, hi in ((2 * p, lo_even, hi_even), (2 * p + 1, lo_odd, hi_odd)):
            v2_ref[kh, row0:row0 + rows, 0:LANES] = lo.astype(BF16)
            v2_ref[kh, row0:row0 + rows, LANES:2 * LANES] = hi.astype(BF16)


def _attend_rows(qz, kband, vband, bias, sink):
    t = qz.shape[0] // GROUP
    s = _dot_nt(qz, kband) + bias
    m = jnp.maximum(jnp.max(s, axis=-1, keepdims=True), sink)
    p = jnp.exp(s - m).astype(BF16)
    o2 = _dot(p, vband)
    inv = 1.0 / (_dot(p, jnp.ones((p.shape[1], LANES), BF16)) + jnp.exp(sink - m))
    o2 = o2 * jnp.concatenate([inv, inv], axis=1)
    return (o2[0:t, 0:LANES] + o2[t:2 * t, LANES:2 * LANES],
            o2[2 * t:3 * t, 0:LANES] + o2[3 * t:4 * t, LANES:2 * LANES])


def _attn_sample_body(x_ref, ck_ref, cv_ref, g_ref, wqkv_ref, bqkv_ref, wo_ref, bias_ref, sink_ref,
                      o_ref, nk_ref, nv_ref, qz_ref, kp_ref, v2_ref, att_ref):
    x = x_ref[...]
    h = _rmsnorm(x, g_ref[...]).astype(BF16)
    qkv = _dot(h, wqkv_ref[...]) + bqkv_ref[...]
    q = qkv[:, :Q_DIM] * (HEAD_DIM ** -0.5)
    k = qkv[:, Q_DIM:Q_DIM + KV_DIM]
    v = qkv[:, Q_DIM + KV_DIM:]
    for b in range(DEC_BATCH):
        rows = slice(b * DEC_SEQ, (b + 1) * DEC_SEQ)
        ck = ck_ref[b]
        cv = cv_ref[b]
        nk_ref[b, 0:WINDOW - DEC_SEQ, :] = ck[DEC_SEQ:]
        nk_ref[b, WINDOW - DEC_SEQ:, :] = k[rows]
        nv_ref[b, 0:WINDOW - DEC_SEQ, :] = cv[DEC_SEQ:]
        nv_ref[b, WINDOW - DEC_SEQ:, :] = v[rows]
        _stage_q(q[rows], qz_ref.at[b], DEC_SEQ)
        _stage_kv(ck, cv, kp_ref.at[b], v2_ref.at[b], 0)
        _stage_kv(k[rows], v[rows], kp_ref.at[b], v2_ref.at[b], WINDOW)
        pairs = []
        for kh in range(N_KV_HEADS):
            pairs.extend(_attend_rows(qz_ref[b, kh, 0], kp_ref[b, kh // 2], v2_ref[b, kh], bias_ref[kh], sink_ref[kh]))
        att_ref[rows, :] = jnp.concatenate(pairs, axis=1).astype(BF16)
    o_ref[...] = x + _dot(att_ref[...], wo_ref[...])


def _attn_sample(x, ck, cv, g, w_qkv, b_qkv, w_o, bias, sinks, layer, a):
    nkeys = WINDOW + DEC_SEQ
    kv_shape = jax.ShapeDtypeStruct((DEC_BATCH, WINDOW, KV_DIM), F32)
    kv_spec = pl.BlockSpec((DEC_BATCH, WINDOW, KV_DIM), lambda i: (0, 0, 0))
    x_spec = pl.BlockSpec((SAMPLE_ROWS, D_MODEL), lambda i: (SAMPLE_BLOCK, 0))
    return pl.pallas_call(
        _attn_sample_body,
        grid=(1,),
        in_specs=[
            x_spec,
            _layer_spec((DEC_BATCH, WINDOW, KV_DIM), a),
            _layer_spec((DEC_BATCH, WINDOW, KV_DIM), a),
            _layer_spec((1, D_MODEL), layer),
            _layer_spec((D_MODEL, QKV_DIM), a),
            _layer_spec((1, QKV_DIM), a),
            _layer_spec((Q_DIM, D_MODEL), a),
            _const_spec((N_KV_HEADS, GROUP * DEC_SEQ, nkeys)),
            _layer_spec((N_KV_HEADS, GROUP * DEC_SEQ, 1), a),
        ],
        out_specs=[x_spec, kv_spec, kv_spec],
        out_shape=[jax.ShapeDtypeStruct((ROWS, D_MODEL), F32), kv_shape, kv_shape],
        scratch_shapes=[
            pltpu.VMEM((DEC_BATCH, N_KV_HEADS, 1, GROUP * DEC_SEQ, LANES), BF16),
            pltpu.VMEM((DEC_BATCH, KV_DIM // LANES, nkeys, LANES), BF16),
            pltpu.VMEM((DEC_BATCH, N_KV_HEADS, nkeys, 2 * LANES), BF16),
            pltpu.VMEM((SAMPLE_ROWS, Q_DIM), BF16),
        ],
        input_output_aliases={0: 0},
        compiler_params=_params(("arbitrary",)),
        name="attn_sample",
    )(x, ck, cv, g, w_qkv, b_qkv, w_o, bias, sinks)


GLU_BLOCKS = 2


def _glu_cols(h, c2, win_ref, bin_ref):
    width = GLU_BLOCKS * LANES
    val = slice(c2 * width, (c2 + 1) * width)
    gate = slice(D_MODEL + c2 * width, D_MODEL + (c2 + 1) * width)
    a = _dot(h, win_ref[:, val].astype(BF16)) + bin_ref[:, val]
    g = _dot(h, win_ref[:, gate].astype(BF16)) + bin_ref[:, gate]
    return a * jax.nn.sigmoid(g)


def _conv_cols(ubuf_ref, c, src0, dst0, rb, wdw_ref, bdw_ref, conv_ref):
    cols = slice(c * LANES, (c + 1) * LANES)
    acc = jnp.broadcast_to(bdw_ref[:, cols], (rb, LANES))
    for w in range(CONV_WIDTH):
        tap0 = src0 + HIST - CONV_PAD + w
        acc = acc + ubuf_ref[c, tap0:tap0 + rb, :] * wdw_ref[w:w + 1, cols]
    conv_ref[dst0:dst0 + rb, cols] = acc


def _ln_silu(r0, rb, lng_ref, lnb_ref, conv_ref, y_ref):
    yc = conv_ref[r0:r0 + rb, :]
    xc = yc - jnp.mean(yc, axis=-1, keepdims=True)
    yn = xc * lax.rsqrt(jnp.mean(xc * xc, axis=-1, keepdims=True) + NORM_EPS)
    yn = yn * lng_ref[...] + lnb_ref[...]
    y_ref[r0:r0 + rb, :] = (yn * jax.nn.sigmoid(yn)).astype(BF16)


def _conv_prompt_body(x_ref, g_ref, win_ref, bin_ref, wdw_ref, bdw_ref, lng_ref, lnb_ref, wout_ref,
                      bout_ref, o_ref, st_ref, ubuf_ref, conv_ref, y_ref):
    @pl.when(pl.program_id(1) == 0)
    def _():
        ubuf_ref[:, 0:HIST, :] = jnp.zeros((D_MODEL // LANES, HIST, LANES), F32)

    x = x_ref[...]
    h = _rmsnorm(x, g_ref[...]).astype(BF16)
    for c2 in range(D_MODEL // LANES // GLU_BLOCKS):
        u2 = _glu_cols(h, c2, win_ref, bin_ref)
        for j in range(GLU_BLOCKS):
            c = c2 * GLU_BLOCKS + j
            u = u2[:, j * LANES:(j + 1) * LANES]
            st_ref[:, c * LANES:(c + 1) * LANES] = u[MIX_TILE - HIST:]
            ubuf_ref[c, HIST:, :] = u
        for j in range(GLU_BLOCKS):
            for r0 in range(0, MIX_TILE, CONV_ROWS):
                _conv_cols(ubuf_ref, c2 * GLU_BLOCKS + j, r0, r0, CONV_ROWS, wdw_ref, bdw_ref, conv_ref)
    half = MIX_TILE // 2
    for r in (0, half):
        for r0 in range(r, r + half, CONV_ROWS):
            _ln_silu(r0, CONV_ROWS, lng_ref, lnb_ref, conv_ref, y_ref)
        o_ref[r:r + half, :] = (x[r:r + half] + _dot(y_ref[r:r + half, :], wout_ref[...].astype(BF16))
                                + bout_ref[...])
    ubuf_ref[:, 0:HIST, :] = ubuf_ref[:, MIX_TILE:MIX_TILE + HIST, :]


def _conv_sample_body(x_ref, hist_ref, g_ref, win_ref, bin_ref, wdw_ref, bdw_ref, lng_ref, lnb_ref,
                      wout_ref, bout_ref, o_ref, st_ref, ubuf_ref, conv_ref, y_ref):
    x = x_ref[...]
    h = _rmsnorm(x, g_ref[...]).astype(BF16)
    for c2 in range(D_MODEL // LANES // GLU_BLOCKS):
        u2 = _glu_cols(h, c2, win_ref, bin_ref)
        for j in range(GLU_BLOCKS):
            c = c2 * GLU_BLOCKS + j
            cols = slice(c * LANES, (c + 1) * LANES)
            for b in range(DEC_BATCH):
                rows = slice(b * DEC_SEQ, (b + 1) * DEC_SEQ)
                st_ref[b, :, cols] = u2[rows, j * LANES:(j + 1) * LANES]
                ubuf_ref[b, c, 0:HIST, :] = hist_ref[b, :, cols]
                ubuf_ref[b, c, HIST:, :] = u2[rows, j * LANES:(j + 1) * LANES]
                _conv_cols(ubuf_ref.at[b], c, 0, b * DEC_SEQ, DEC_SEQ, wdw_ref, bdw_ref, conv_ref)
    _ln_silu(0, SAMPLE_ROWS, lng_ref, lnb_ref, conv_ref, y_ref)
    o_ref[...] = x + _dot(y_ref[...], wout_ref[...].astype(BF16)) + bout_ref[...]


def _conv_weight_specs(layer, c):
    return [
        _layer_spec((1, D_MODEL), layer),
        _layer_spec((D_MODEL, 2 * D_MODEL), c),
        _layer_spec((1, 2 * D_MODEL), c),
        _layer_spec((CONV_WIDTH, D_MODEL), c),
        _layer_spec((1, D_MODEL), c),
        _layer_spec((1, D_MODEL), c),
        _layer_spec((1, D_MODEL), c),
        _layer_spec((D_MODEL, D_MODEL), c),
        _layer_spec((1, D_MODEL), c),
    ]


def _conv_prompt(x, weights, layer, c):
    tiles = SEQ // MIX_TILE
    return pl.pallas_call(
        _conv_prompt_body,
        grid=(BATCH, tiles),
        in_specs=[pl.BlockSpec((MIX_TILE, D_MODEL), lambda b, t: (b * tiles + t, 0))] + _conv_weight_specs(layer, c),
        out_specs=[
            pl.BlockSpec((MIX_TILE, D_MODEL), lambda b, t: (b * tiles + t, 0)),
            pl.BlockSpec((None, HIST, D_MODEL), lambda b, t: (b, 0, 0)),
        ],
        out_shape=[jax.ShapeDtypeStruct((ROWS, D_MODEL), F32),
                   jax.ShapeDtypeStruct((BATCH, HIST, D_MODEL), F32)],
        scratch_shapes=[
            pltpu.VMEM((D_MODEL // LANES, HIST + MIX_TILE, LANES), F32),
            pltpu.VMEM((MIX_TILE, D_MODEL), F32),
            pltpu.VMEM((MIX_TILE, D_MODEL), BF16),
        ],
        input_output_aliases={0: 0},
        compiler_params=_params(("arbitrary", "arbitrary")),
        name="conv_prompt",
    )(x, *weights)


def _conv_sample(x, hist, weights, layer, c):
    x_spec = pl.BlockSpec((SAMPLE_ROWS, D_MODEL), lambda i: (SAMPLE_BLOCK, 0))
    return pl.pallas_call(
        _conv_sample_body,
        grid=(1,),
        in_specs=[x_spec, _layer_spec((DEC_BATCH, HIST, D_MODEL), c)] + _conv_weight_specs(layer, c),
        out_specs=[x_spec, pl.BlockSpec((DEC_BATCH, HIST, D_MODEL), lambda i: (0, 0, 0))],
        out_shape=[jax.ShapeDtypeStruct((ROWS, D_MODEL), F32),
                   jax.ShapeDtypeStruct((DEC_BATCH, HIST, D_MODEL), F32)],
        scratch_shapes=[
            pltpu.VMEM((DEC_BATCH, D_MODEL // LANES, HIST + DEC_SEQ, LANES), F32),
            pltpu.VMEM((SAMPLE_ROWS, D_MODEL), F32),
            pltpu.VMEM((SAMPLE_ROWS, D_MODEL), BF16),
        ],
        input_output_aliases={0: 0},
        compiler_params=_params(("arbitrary",)),
        name="conv_sample",
    )(x, hist, *weights)


def _alibi_slopes():
    return np.array([2.0 ** (-8.0 * (h + 1) / N_HEADS) for h in range(N_HEADS)], dtype=np.float32)


def _sample_bias():
    i = np.arange(DEC_SEQ)[:, None]
    j = np.arange(WINDOW + DEC_SEQ)[None, :]
    dist = np.abs(i - (j - WINDOW)).astype(np.float32)
    bias = -_alibi_slopes()[:, None, None] * dist
    return jnp.asarray(bias.reshape(N_KV_HEADS, GROUP * DEC_SEQ, WINDOW + DEC_SEQ))


def _pair_bias():
    r = np.arange(PAIR_BAND)[:, None, None]
    cc = np.arange(PAIR // CHUNK)[None, :, None]
    i = np.arange(CHUNK)[None, None, :]
    j = r - cc * CHUNK
    dist = np.abs(i - (j - WINDOW)).astype(np.float32)
    bias = np.float32(LOG2E) * -_alibi_slopes()[:, None, None, None] * dist[None]
    bias = np.where((j >= 0) & (j < BAND), bias, np.float32(NEG_INF)).astype(np.float32)
    bias = bias.reshape(N_KV_HEADS, GROUP, PAIR_BAND, PAIR).transpose(0, 2, 1, 3)
    return jnp.asarray(bias.reshape(N_KV_HEADS, PAIR_BAND, GROUP * PAIR))


def _sink_lanes(sinks):
    s = LOG2E * sinks.astype(F32).reshape(-1, N_KV_HEADS, GROUP)
    return jnp.repeat(s, PAIR, axis=2)[:, :, None, :]


def _sink_rows(sinks):
    s = sinks.astype(F32).reshape(-1, N_KV_HEADS, GROUP)
    return jnp.repeat(s, DEC_SEQ, axis=2)[:, :, :, None]


def _rows(a):
    return a.astype(F32).reshape(a.shape[0], 1, a.shape[-1])


def _conv_params(w_in, b_in, w_dw, b_dw, ln_g, ln_b, w_out, b_out):
    return (w_in.astype(F32), _rows(b_in), w_dw.astype(F32), _rows(b_dw), _rows(ln_g), _rows(ln_b),
            w_out.astype(F32), _rows(b_out))


def kernel(x_prompt, x_sample, cache_k, cache_v, state_conv, norm_ffn1, norm_mix, norm_ffn2, norm_final,
           ffn1_w_in, ffn1_w_out, ffn2_w_in, ffn2_w_out, attn_w_qkv, attn_b_qkv, attn_w_o, attn_sinks,
           conv_w_in, conv_b_in, conv_w_dw, conv_b_dw, conv_ln_g, conv_ln_b, conv_w_out, conv_b_out):
    xs = (x_prompt.reshape(PROMPT_ROWS, D_MODEL), x_sample.reshape(SAMPLE_ROWS, D_MODEL))
    g_ffn1, g_mix, g_ffn2 = _rows(norm_ffn1), _rows(norm_mix), _rows(norm_ffn2)
    g_final = norm_final.astype(F32).reshape(1, D_MODEL)
    attn_w = (attn_w_qkv.astype(BF16), _rows(attn_b_qkv), attn_w_o.astype(BF16))
    bias_p, sinks_p = _pair_bias(), _sink_lanes(attn_sinks)
    bias_s, sinks_s = _sample_bias(), _sink_rows(attn_sinks)
    ck = cache_k.reshape(-1, DEC_BATCH, WINDOW, KV_DIM)
    cv = cache_v.reshape(-1, DEC_BATCH, WINDOW, KV_DIM)
    conv_w = _conv_params(conv_w_in, conv_b_in, conv_w_dw, conv_b_dw, conv_ln_g, conv_ln_b, conv_w_out, conv_b_out)
    hist = jnp.pad(state_conv, ((0, 0), (0, 0), (HIST - CONV_PAD, 0), (0, 0)))

    new_k_p, new_v_p, new_k_s, new_v_s, new_c_p, new_c_s = [], [], [], [], [], []
    for layer in range(DEPTH):
        x = _ffn(xs if layer == 0 else (x,), g_ffn1, ffn1_w_in, ffn1_w_out, layer, g_final, first=layer == 0)
        if layer % 2 == 0:
            a = layer // 2
            x, nk, nv = _attn_prompt(x, g_mix, *attn_w, bias_p, sinks_p, layer, a)
            new_k_p.append(nk)
            new_v_p.append(nv)
            x, nk, nv = _attn_sample(x, ck, cv, g_mix, *attn_w, bias_s, sinks_s, layer, a)
            new_k_s.append(nk)
            new_v_s.append(nv)
        else:
            c = layer // 2
            x, st = _conv_prompt(x, (g_mix,) + conv_w, layer, c)
            new_c_p.append(st[:, HIST - CONV_PAD:])
            x, st = _conv_sample(x, hist, (g_mix,) + conv_w, layer, c)
            new_c_s.append(st[:, HIST - CONV_PAD:])
        x = _ffn((x,), g_ffn2, ffn2_w_in, ffn2_w_out, layer, g_final, last=layer == DEPTH - 1)

    y_prompt, y_sample = x
    kv5 = lambda parts, b: jnp.stack(parts).reshape(len(parts), b, WINDOW, N_KV_HEADS, HEAD_DIM)
    return (y_prompt.reshape(BATCH, SEQ, D_MODEL),
            y_sample.reshape(DEC_BATCH, DEC_SEQ, D_MODEL),
            kv5(new_k_p, BATCH), kv5(new_v_p, BATCH), jnp.stack(new_c_p),
            kv5(new_k_s, DEC_BATCH), kv5(new_v_s, DEC_BATCH), jnp.stack(new_c_s))
```
